```python
import jax
import jax.numpy as jnp
from jax import lax
import numpy as np

D_MODEL = 1024
BATCH = 4
SEQ = 8192
DEPTH = 2

CHUNK = 64
Q_BLOCK = 128
EPS = 1e-6
ROPE_THETA = 10000.0

MLA_HEADS = 8
MLA_Q_RANK = 384
MLA_KV_RANK = 256
MLA_NOPE = 64
MLA_ROPE = 32
MLA_V = 64

CONV_WIDTH = 512
CONV_K = 3

RET_HEADS = 4
RET_DK = 128
RET_DV = 256

N_GROUPS = 4
EXPERTS_PER_GROUP = 8
N_EXPERTS = N_GROUPS * EXPERTS_PER_GROUP
EXPERT_FF = 512
TOP_K = 2

N_BRANCHES = 3
IN_SIZES = (MLA_Q_RANK, MLA_KV_RANK, MLA_ROPE, CONV_WIDTH, CONV_WIDTH, CONV_WIDTH, RET_HEADS * RET_DK, RET_HEADS * RET_DK, RET_HEADS * RET_DV, RET_HEADS * RET_DV, N_BRANCHES * D_MODEL)
IN_TOTAL = sum(IN_SIZES)

kernel_name = 'hybrid_mla_shortconv_retention_hmoe'


def _rmsnorm(x, g):
    xf = x.astype(jnp.float32)
    y = xf * lax.rsqrt(jnp.mean(xf * xf, axis=-1, keepdims=True) + EPS)
    return (y * g.astype(jnp.float32)).astype(x.dtype)


def _rope_tables(positions, dim):
    inv = ROPE_THETA ** (-jnp.arange(0, dim, 2, dtype=jnp.float32) / dim)
    ang = positions.astype(jnp.float32)[..., None] * inv
    return jnp.cos(ang)[:, :, None, :], jnp.sin(ang)[:, :, None, :]


def _apply_rope(x, cos, sin):
    xf = x.astype(jnp.float32)
    x1, x2 = jnp.split(xf, 2, axis=-1)
    return jnp.concatenate([x1 * cos - x2 * sin, x1 * sin + x2 * cos], axis=-1).astype(x.dtype)


def _split_cols(a, sizes):
    outs = []
    start = 0
    for size in sizes:
        outs.append(a[..., start:start + size])
        start += size
    return outs


def _mla_branch(q_lat, kv_lat, k_rope, q_norm_g, kv_norm_g, w_uq, w_ukv, w_o, cos, sin):
    b, s, _ = q_lat.shape
    dq = MLA_NOPE + MLA_ROPE
    q = (_rmsnorm(q_lat, q_norm_g) @ w_uq).reshape(b, s, MLA_HEADS, dq)
    q = jnp.concatenate([q[..., :MLA_NOPE], _apply_rope(q[..., MLA_NOPE:], cos, sin)], axis=-1)
    kv = (_rmsnorm(kv_lat, kv_norm_g) @ w_ukv).reshape(b, s, MLA_HEADS, MLA_NOPE + MLA_V)
    k_nope, v = kv[..., :MLA_NOPE], kv[..., MLA_NOPE:]
    k_pe = _apply_rope(k_rope[:, :, None, :], cos, sin)
    k = jnp.concatenate([k_nope, jnp.broadcast_to(k_pe, (b, s, MLA_HEADS, MLA_ROPE))], axis=-1)
    scale = dq ** -0.5
    n_blocks = s // Q_BLOCK
    q_blocks = q.reshape(b, n_blocks, Q_BLOCK, MLA_HEADS, dq).swapaxes(0, 1)
    key_chunk = jnp.arange(s) // CHUNK

    def attend(args):
        q_blk, blk = args
        q_chunk = (blk * Q_BLOCK + jnp.arange(Q_BLOCK)) // CHUNK
        sc = jnp.einsum('bqhd,bkhd->bhqk', q_blk, k, preferred_element_type=jnp.float32) * scale
        sc = jnp.where(key_chunk[None, :] <= q_chunk[:, None], sc, -jnp.inf)
        p = jax.nn.softmax(sc, axis=-1)
        return jnp.einsum('bhqk,bkhd->bqhd', p.astype(v.dtype), v)

    o = lax.map(attend, (q_blocks, jnp.arange(n_blocks)))
    o = o.swapaxes(0, 1).reshape(b, s, MLA_HEADS * MLA_V)
    return o @ w_o


def _shortconv_branch(b_gate, c_gate, xv, conv_w, w_o):
    u = c_gate * xv
    y = lax.conv_general_dilated(u, conv_w[:, None, :], window_strides=(1,), padding=((CONV_K - 1, 0),), dimension_numbers=('NWC', 'WIO', 'NWC'), feature_group_count=CONV_WIDTH)
    return (b_gate * y) @ w_o


def _retention_branch(q, k, v, g, w_o, cos, sin):
    b, s, _ = q.shape
    n = s // CHUNK
    q = _apply_rope(q.reshape(b, s, RET_HEADS, RET_DK), cos, sin).astype(jnp.float32)
    k = _apply_rope(k.reshape(b, s, RET_HEADS, RET_DK), cos, sin).astype(jnp.float32) * (RET_DK ** -0.5)
    v = v.reshape(b, s, RET_HEADS, RET_DV).astype(jnp.float32)
    log_gamma = jnp.log(1.0 - 2.0 ** (-5.0 - jnp.arange(RET_HEADS, dtype=jnp.float32)))
    pos = jnp.arange(CHUNK, dtype=jnp.float32)
    intra_decay = jnp.exp(log_gamma[:, None, None] * jnp.abs(pos[:, None] - pos[None, :]))[None]
    q_decay = jnp.exp(log_gamma[:, None] * (pos + 1.0))[None, :, :, None]
    k_decay = jnp.exp(log_gamma[:, None] * (CHUNK - 1.0 - pos))[None, :, :, None]
    chunk_decay = jnp.exp(log_gamma * CHUNK)[None, :, None, None]

    def to_chunks(a):
        return a.reshape(b, n, CHUNK, RET_HEADS, a.shape[-1]).transpose(1, 0, 3, 2, 4)

    def step(state, inp):
        qc, kc, vc = inp
        sc = jnp.einsum('bhid,bhjd->bhij', qc, kc) * intra_decay
        o = jnp.einsum('bhij,bhjv->bhiv', sc, vc) + jnp.einsum('bhid,bhdv->bhiv', qc, state) * q_decay
        state = state * chunk_decay + jnp.einsum('bhjd,bhjv->bhdv', kc * k_decay, vc)
        return state, o

    state0 = jnp.zeros((b, RET_HEADS, RET_DK, RET_DV), jnp.float32)
    _, o = lax.scan(step, state0, (to_chunks(q), to_chunks(k), to_chunks(v)))
    o = o.transpose(1, 0, 3, 2, 4).reshape(b, s, RET_HEADS, RET_DV)
    o = o * lax.rsqrt(jnp.mean(o * o, axis=-1, keepdims=True) + EPS)
    o = o.reshape(b, s, RET_HEADS * RET_DV).astype(g.dtype)
    return (jax.nn.silu(g) * o) @ w_o


def _hier_moe(h, w_rg, b_rg, w_re, b_re, w_gate, w_up, w_down):
    b, s, d = h.shape
    t = h.reshape(b * s, d)
    n_tok = t.shape[0]
    g_prob = jax.nn.softmax((t @ w_rg).astype(jnp.float32) + b_rg.astype(jnp.float32), axis=-1)
    g_val, g_idx = lax.top_k(g_prob, 1)
    e_logits = ((t @ w_re).astype(jnp.float32) + b_re.astype(jnp.float32)).reshape(n_tok, N_GROUPS, EXPERTS_PER_GROUP)
    e_logits = jnp.take_along_axis(e_logits, g_idx[:, :, None], axis=1)[:, 0]
    e_val, e_idx = lax.top_k(jax.nn.softmax(e_logits, axis=-1), TOP_K)
    e_val = e_val / jnp.sum(e_val, axis=-1, keepdims=True)
    weights = g_val * e_val
    expert_id = g_idx * EXPERTS_PER_GROUP + e_idx
    combine = jnp.einsum('tk,tke->et', weights, jax.nn.one_hot(expert_id, N_EXPERTS, dtype=jnp.float32))

    def expert(acc, inp):
        wg, wu, wd, gate = inp
        hid = jax.nn.silu(t @ wg) * (t @ wu)
        return acc + gate[:, None] * (hid @ wd).astype(jnp.float32), None

    out, _ = lax.scan(expert, jnp.zeros((n_tok, d), jnp.float32), (w_gate, w_up, w_down, combine))
    return out.reshape(b, s, d).astype(h.dtype)


def _normal(key, shape, scale):
    return jax.random.normal(key, shape, jnp.float32) * scale


def setup_inputs(seed: int = 0) -> dict:
    key = jax.random.key(seed)
    ks = jax.random.split(key, 32)
    D = D_MODEL
    x = _normal(ks[0], (BATCH, SEQ, D), 1.0)
    c = _normal(ks[1], (BATCH, D), 1.0)
    positions = jax.random.randint(ks[2], (BATCH, 1), 0, 4096, dtype=jnp.int32) + jnp.arange(SEQ, dtype=jnp.int32)[None, :]
    return {
        'x': x,
        'c': c,
        'positions': positions,
        'w_ada': _normal(ks[3], (DEPTH, D, 6 * D), 0.5 * D ** -0.5),
        'b_ada': _normal(ks[4], (DEPTH, 6 * D), 0.02),
        'norm_mix_g': 1.0 + _normal(ks[5], (DEPTH, D), 0.05),
        'norm_ffn_g': 1.0 + _normal(ks[6], (DEPTH, D), 0.05),
        'w_in': _normal(ks[7], (DEPTH, D, IN_TOTAL), D ** -0.5),
        'mla_q_norm_g': 1.0 + _normal(ks[8], (DEPTH, MLA_Q_RANK), 0.05),
        'mla_kv_norm_g': 1.0 + _normal(ks[9], (DEPTH, MLA_KV_RANK), 0.05),
        'w_uq': _normal(ks[10], (DEPTH, MLA_Q_RANK, MLA_HEADS * (MLA_NOPE + MLA_ROPE)), MLA_Q_RANK ** -0.5),
        'w_ukv': _normal(ks[11], (DEPTH, MLA_KV_RANK, MLA_HEADS * (MLA_NOPE + MLA_V)), MLA_KV_RANK ** -0.5),
        'w_o_mla': _normal(ks[12], (DEPTH, MLA_HEADS * MLA_V, D), (MLA_HEADS * MLA_V) ** -0.5),
        'conv_w': _normal(ks[13], (DEPTH, CONV_K, CONV_WIDTH), CONV_K ** -0.5),
        'w_o_conv': _normal(ks[14], (DEPTH, CONV_WIDTH, D), CONV_WIDTH ** -0.5),
        'w_o_ret': _normal(ks[15], (DEPTH, RET_HEADS * RET_DV, D), (RET_HEADS * RET_DV) ** -0.5),
        'w_mix_out': _normal(ks[16], (DEPTH, D, D), D ** -0.5),
        'w_route_group': _normal(ks[17], (DEPTH, D, N_GROUPS), D ** -0.5),
        'b_route_group': _normal(ks[18], (DEPTH, N_GROUPS), 0.01),
        'w_route_expert': _normal(ks[19], (DEPTH, D, N_EXPERTS), D ** -0.5),
        'b_route_expert': _normal(ks[20], (DEPTH, N_EXPERTS), 0.01),
        'w_exp_gate': _normal(ks[21], (DEPTH, N_EXPERTS, D, EXPERT_FF), D ** -0.5),
        'w_exp_up': _normal(ks[22], (DEPTH, N_EXPERTS, D, EXPERT_FF), D ** -0.5),
        'w_exp_down': _normal(ks[23], (DEPTH, N_EXPERTS, EXPERT_FF, D), EXPERT_FF ** -0.5),
        'final_g': 1.0 + _normal(ks[24], (D,), 0.05),
    }


def reference(x, c, positions, w_ada, b_ada, norm_mix_g, norm_ffn_g, w_in, mla_q_norm_g, mla_kv_norm_g, w_uq, w_ukv, w_o_mla, conv_w, w_o_conv, w_o_ret, w_mix_out, w_route_group, b_route_group, w_route_expert, b_route_expert, w_exp_gate, w_exp_up, w_exp_down, final_g):
    b, s, d = x.shape
    mla_cos, mla_sin = _rope_tables(positions, MLA_ROPE)
    ret_cos, ret_sin = _rope_tables(positions, RET_DK)
    c_act = jax.nn.silu(c)
    for l in range(DEPTH):
        mod = c_act @ w_ada[l] + b_ada[l]
        sh_mix, sc_mix, gt_mix, sh_ffn, sc_ffn, gt_ffn = [m[:, None, :] for m in jnp.split(mod, 6, axis=-1)]
        h = _rmsnorm(x, norm_mix_g[l]) * (1.0 + sc_mix) + sh_mix
        q_lat, kv_lat, k_rope, cb, cc, cx, rq, rk, rv, rg, gl = _split_cols(h @ w_in[l], IN_SIZES)
        y_mla = _mla_branch(q_lat, kv_lat, k_rope, mla_q_norm_g[l], mla_kv_norm_g[l], w_uq[l], w_ukv[l], w_o_mla[l], mla_cos, mla_sin)
        y_conv = _shortconv_branch(cb, cc, cx, conv_w[l], w_o_conv[l])
        y_ret = _retention_branch(rq, rk, rv, rg, w_o_ret[l], ret_cos, ret_sin)
        gates = jax.nn.sigmoid(gl).reshape(b, s, N_BRANCHES, d)
        merged = gates[:, :, 0] * y_mla + gates[:, :, 1] * y_conv + gates[:, :, 2] * y_ret
        x = x + gt_mix * (merged @ w_mix_out[l])
        h = _rmsnorm(x, norm_ffn_g[l]) * (1.0 + sc_ffn) + sh_ffn
        x = x + gt_ffn * _hier_moe(h, w_route_group[l], b_route_group[l], w_route_expert[l], b_route_expert[l], w_exp_gate[l], w_exp_up[l], w_exp_down[l])
    return _rmsnorm(x, final_g)
```

```python
import functools

import jax
import jax.numpy as jnp
from jax import lax
from jax.experimental import pallas as pl
from jax.experimental.pallas import tpu as pltpu

F32 = jnp.float32
BF16 = jnp.bfloat16

D_MODEL = 1024
CHUNK = 64
EPS = 1e-6
ROPE_THETA = 10000.0

MLA_HEADS = 8
MLA_Q_RANK = 384
MLA_KV_RANK = 256
MLA_NOPE = 64
MLA_ROPE = 32
MLA_V = 64

CONV_WIDTH = 512
CONV_K = 3

RET_HEADS = 4
RET_DK = 128
RET_DV = 256

N_GROUPS = 4
EXPERTS_PER_GROUP = 8
N_EXPERTS = N_GROUPS * EXPERTS_PER_GROUP
EXPERT_FF = 512

LANES = 128
SUBLANES = 8
VMEM_LIMIT = 52 * 1024 * 1024

GATE_W = 3 * D_MODEL
RQ_OFF = GATE_W
RK_OFF = RQ_OFF + RET_HEADS * RET_DK
RV_OFF = RK_OFF + RET_HEADS * RET_DK
RG_OFF = RV_OFF + RET_HEADS * RET_DV
CONV_OFF = RG_OFF + RET_HEADS * RET_DV
MLA_OFF = CONV_OFF + 3 * CONV_WIDTH
MLA_W = MLA_Q_RANK + MLA_KV_RANK + LANES
PROJ_W = MLA_OFF + MLA_W
PROJ_TN = PROJ_W // 6
assert PROJ_TN * 6 == PROJ_W and PROJ_TN % LANES == 0
assert MLA_OFF % MLA_W == 0 and CONV_OFF % (3 * CONV_WIDTH) == 0

QPAD = LANES
ROUTE_E_OFF = 32

ROW_TILE_PROJ = 1024
ROW_TILE = 256
ATT_TILE = 512
RET_BLOCK = 256
MOE_TILE = 256
ADA_TN = 768


def _params(sem, vmem=VMEM_LIMIT):
    return pltpu.CompilerParams(dimension_semantics=sem, vmem_limit_bytes=vmem)


def _const_spec(shape):
    nd = len(shape)
    return pl.BlockSpec(shape, lambda *_: (0,) * nd)


def _rms(x, g):
    return x * lax.rsqrt(jnp.mean(x * x, axis=-1, keepdims=True) + EPS) * g


def _sigmoid(x):
    return 1.0 / (1.0 + jnp.exp(-x))


def _ada_kernel(ct_ref, w_ref, b_ref, o_ref):
    ct = ct_ref[...]
    a = ct * _sigmoid(ct)
    w = w_ref[0]
    rows = [jnp.sum(w * a[:, b:b + 1], axis=0, keepdims=True) for b in range(ct.shape[1])]
    o_ref[0] = jnp.concatenate(rows, axis=0) + b_ref[0]


def _ada_mod(c, w_ada, b_ada):
    depth, d, n = w_ada.shape
    b = c.shape[0]
    return pl.pallas_call(
        _ada_kernel,
        grid=(depth, n // ADA_TN),
        in_specs=[
            _const_spec((d, b)),
            pl.BlockSpec((1, d, ADA_TN), lambda l, j: (l, 0, j)),
            pl.BlockSpec((1, 1, ADA_TN), lambda l, j: (l, 0, j)),
        ],
        out_specs=pl.BlockSpec((1, b, ADA_TN), lambda l, j: (l, 0, j)),
        out_shape=jax.ShapeDtypeStruct((depth, b, n), F32),
        compiler_params=_params(("arbitrary", "arbitrary")),
        name="ada_mod",
    )(c.T, w_ada, b_ada.reshape(depth, 1, n))


def _rope_kernel(pos_ref, f_ref, rc_ref, rs_ref, mc_ref, ms_ref):
    ang = pos_ref[...].astype(F32) * f_ref[...]
    cs = jnp.cos(ang)
    sn = jnp.sin(ang)
    lane = lax.broadcasted_iota(jnp.int32, ang.shape, 1)
    half = LANES // 2
    lo = lane < half
    rc_ref[...] = jnp.where(lo, cs, pltpu.roll(cs, half, 1))
    rs_ref[...] = jnp.where(lo, -sn, pltpu.roll(sn, half, 1))
    rope = (lane >= MLA_NOPE) & (lane < MLA_NOPE + MLA_ROPE)
    mc_ref[...] = jnp.where(lane < MLA_NOPE, 1.0, jnp.where(rope, cs, 0.0))
    ms_ref[...] = jnp.where(rope, sn, 0.0)


def _rope_tables(positions):
    t = positions.size
    tm = min(2048, t)
    inv_ret = ROPE_THETA ** (-jnp.arange(0, RET_DK, 2, dtype=F32) / RET_DK)
    inv_mla = ROPE_THETA ** (-jnp.arange(0, MLA_ROPE, 2, dtype=F32) / MLA_ROPE)
    freqs = jnp.concatenate([inv_ret, inv_mla, inv_mla, jnp.zeros((LANES - 96,), F32)]).reshape(1, LANES)
    spec = pl.BlockSpec((tm, LANES), lambda i: (i, 0))
    return pl.pallas_call(
        _rope_kernel,
        grid=(t // tm,),
        in_specs=[pl.BlockSpec((tm, 1), lambda i: (i, 0)), _const_spec((1, LANES))],
        out_specs=[spec] * 4,
        out_shape=[jax.ShapeDtypeStruct((t, LANES), F32)] * 4,
        compiler_params=_params(("arbitrary",)),
        name="rope_tables",
    )(positions.reshape(t, 1), freqs)


def _proj_kernel(x_ref, sh_ref, sc_ref, g_ref, w_ref, o_ref, h_scr):
    @pl.when(pl.program_id(1) == 0)
    def _():
        h = _rms(x_ref[...], g_ref[...]) * (1.0 + sc_ref[0]) + sh_ref[0]
        h_scr[...] = h.astype(BF16)

    o_ref[...] = jnp.dot(h_scr[...], w_ref[...], preferred_element_type=F32).astype(BF16)


def _in_proj(x2, mod3, g, w, seq):
    t, d = x2.shape
    tm = min(ROW_TILE_PROJ, seq)
    per_b = seq // tm
    return pl.pallas_call(
        _proj_kernel,
        grid=(t // tm, PROJ_W // PROJ_TN),
        in_specs=[
            pl.BlockSpec((tm, d), lambda i, j: (i, 0)),
            pl.BlockSpec((1, 1, d), lambda i, j: ((i // per_b) * 6 + 0, 0, 0)),
            pl.BlockSpec((1, 1, d), lambda i, j: ((i // per_b) * 6 + 1, 0, 0)),
            _const_spec((1, d)),
            pl.BlockSpec((d, PROJ_TN), lambda i, j: (0, j)),
        ],
        out_specs=pl.BlockSpec((tm, PROJ_TN), lambda i, j: (i, j)),
        out_shape=jax.ShapeDtypeStruct((t, PROJ_W), BF16),
        scratch_shapes=[pltpu.VMEM((tm, d), BF16)],
        compiler_params=_params(("arbitrary", "arbitrary")),
        name="in_proj",
    )(x2, mod3, mod3, g, w)


def _mla_prep_kernel(in_ref, mc_ref, ms_ref, gq_ref, gkv_ref, wa_ref, wb_ref, wk_ref, wv_ref, q_ref, k_ref, v_ref):
    blk = in_ref[...].astype(F32)
    qn = _rms(blk[:, :MLA_Q_RANK], gq_ref[...]).astype(BF16)
    kvn = _rms(blk[:, MLA_Q_RANK:MLA_Q_RANK + MLA_KV_RANK], gkv_ref[...]).astype(BF16)
    kb = blk[:, MLA_Q_RANK + MLA_KV_RANK:]
    mc = mc_ref[...]
    ms = ms_ref[...]
    qa = jnp.dot(qn, wa_ref[...], preferred_element_type=F32)
    qb = jnp.dot(qn, wb_ref[...], preferred_element_type=F32)
    kn = jnp.dot(kvn, wk_ref[...], preferred_element_type=F32)
    lane = lax.broadcasted_iota(jnp.int32, kb.shape, 1)
    rope = (lane >= MLA_NOPE) & (lane < MLA_NOPE + MLA_ROPE)
    kpe = jnp.where(rope, kb * mc + pltpu.roll(kb, LANES // 2, 1) * ms, 0.0)
    scale = (MLA_NOPE + MLA_ROPE) ** -0.5
    for h in range(MLA_HEADS):
        sl = slice(h * QPAD, (h + 1) * QPAD)
        q_ref[:, sl] = ((qa[:, sl] * mc + qb[:, sl] * ms) * scale).astype(BF16)
        k_ref[:, sl] = (kn[:, sl] + kpe).astype(BF16)
    v_ref[...] = jnp.dot(kvn, wv_ref[...], preferred_element_type=F32).astype(BF16)


def _mla_prep(proj, mc, ms, gq, gkv, wa, wb, wk, wv):
    t = proj.shape[0]
    tm = min(512, t)
    hq = MLA_HEADS * QPAD
    hv = MLA_HEADS * MLA_V
    return pl.pallas_call(
        _mla_prep_kernel,
        grid=(t // tm,),
        in_specs=[
            pl.BlockSpec((tm, MLA_W), lambda i: (i, MLA_OFF // MLA_W)),
            pl.BlockSpec((tm, LANES), lambda i: (i, 0)),
            pl.BlockSpec((tm, LANES), lambda i: (i, 0)),
            _const_spec((1, MLA_Q_RANK)),
            _const_spec((1, MLA_KV_RANK)),
            _const_spec((MLA_Q_RANK, hq)),
            _const_spec((MLA_Q_RANK, hq)),
            _const_spec((MLA_KV_RANK, hq)),
            _const_spec((MLA_KV_RANK, hv)),
        ],
        out_specs=[
            pl.BlockSpec((tm, hq), lambda i: (i, 0)),
            pl.BlockSpec((tm, hq), lambda i: (i, 0)),
            pl.BlockSpec((tm, hv), lambda i: (i, 0)),
        ],
        out_shape=[
            jax.ShapeDtypeStruct((t, hq), BF16),
            jax.ShapeDtypeStruct((t, hq), BF16),
            jax.ShapeDtypeStruct((t, hv), BF16),
        ],
        compiler_params=_params(("arbitrary",)),
        name="mla_prep",
    )(proj, mc, ms, gq, gkv, wa, wb, wk, wv)


def _attn_kernel(q_ref, k_ref, v_ref, o_ref, m_scr, l_scr, acc_scr, *, tile):
    qi = pl.program_id(2)
    row = lax.broadcasted_iota(jnp.int32, (tile, tile), 0)
    col = lax.broadcasted_iota(jnp.int32, (tile, tile), 1)
    diag_mask = (col // CHUNK) <= (row // CHUNK)

    m_scr[...] = jnp.full(m_scr.shape, -jnp.inf, F32)
    l_scr[...] = jnp.zeros(l_scr.shape, F32)
    acc_scr[...] = jnp.zeros(acc_scr.shape, F32)

    def step(j, masked):
        start = pl.multiple_of(j * tile, tile)
        v = v_ref[0, pl.ds(start, tile), :]
        for a in range(2):
            q = q_ref[0, :, a * QPAD:(a + 1) * QPAD]
            k = k_ref[0, pl.ds(start, tile), a * QPAD:(a + 1) * QPAD]
            s = lax.dot_general(q, k, (((1,), (1,)), ((), ())), preferred_element_type=F32)
            if masked:
                s = jnp.where(diag_mask, s, -jnp.inf)
            m_old = m_scr[a]
            m_new = jnp.maximum(m_old, jnp.max(s, axis=-1, keepdims=True))
            p = jnp.exp(s - m_new)
            alpha = jnp.exp(m_old - m_new)
            l_scr[a] = alpha * l_scr[a] + jnp.sum(p, axis=-1, keepdims=True)
            acc_scr[a] = alpha * acc_scr[a] + jnp.dot(p.astype(BF16), v, preferred_element_type=F32)
            m_scr[a] = m_new

    def body(j, carry):
        step(j, False)
        return carry

    lax.fori_loop(0, qi, body, 0)
    step(qi, True)

    lane = lax.broadcasted_iota(jnp.int32, (tile, LANES), 1)
    o0 = acc_scr[0] / l_scr[0]
    o1 = acc_scr[1] / l_scr[1]
    o_ref[0] = jnp.where(lane < MLA_V, o0, o1).astype(BF16)


def _attention(q, k, v, batch, seq):
    tile = min(ATT_TILE, seq)
    q3 = q.reshape(batch, seq, MLA_HEADS * QPAD)
    k3 = k.reshape(batch, seq, MLA_HEADS * QPAD)
    v3 = v.reshape(batch, seq, MLA_HEADS * MLA_V)
    out = pl.pallas_call(
        functools.partial(_attn_kernel, tile=tile),
        grid=(batch, MLA_HEADS // 2, seq // tile),
        in_specs=[
            pl.BlockSpec((1, tile, 2 * QPAD), lambda b, h, i: (b, i, h)),
            pl.BlockSpec((1, seq, 2 * QPAD), lambda b, h, i: (b, 0, h)),
            pl.BlockSpec((1, seq, 2 * MLA_V), lambda b, h, i: (b, 0, h)),
        ],
        out_specs=pl.BlockSpec((1, tile, 2 * MLA_V), lambda b, h, i: (b, i, h)),
        out_shape=jax.ShapeDtypeStruct((batch, seq, MLA_HEADS * MLA_V), BF16),
        scratch_shapes=[
            pltpu.VMEM((2, tile, 1), F32),
            pltpu.VMEM((2, tile, 1), F32),
            pltpu.VMEM((2, tile, 2 * MLA_V), F32),
        ],
        compiler_params=_params(("arbitrary", "arbitrary", "arbitrary")),
        name="mla_attention",
    )(q3, k3, v3)
    return out.reshape(batch * seq, MLA_HEADS * MLA_V)


def _ret_kernel(lg_ref, q_ref, k_ref, v_ref, g_ref, rc_ref, rs_ref, o_ref, state_scr, decay_scr, *, blk):
    h = pl.program_id(1)
    n = pl.program_id(2)
    lg = lg_ref[h]

    @pl.when(n == 0)
    def _():
        state_scr[...] = jnp.zeros(state_scr.shape, F32)
        row = lax.broadcasted_iota(jnp.int32, (blk, blk), 0)
        col = lax.broadcasted_iota(jnp.int32, (blk, blk), 1)
        dist = jnp.abs(row - col).astype(F32)
        decay_scr[...] = jnp.where((col // CHUNK) <= (row // CHUNK), jnp.exp(lg * dist), 0.0)

    rc = rc_ref[...]
    rs = rs_ref[...]
    half = RET_DK // 2
    qf = q_ref[...].astype(F32)
    kf = k_ref[...].astype(F32)
    q = qf * rc + pltpu.roll(qf, half, 1) * rs
    k = (kf * rc + pltpu.roll(kf, half, 1) * rs) * (RET_DK ** -0.5)
    v = v_ref[...]

    pos = lax.broadcasted_iota(jnp.int32, (blk, 1), 0).astype(F32)
    q_dec = jnp.exp(lg * (pos + 1.0))
    k_dec = jnp.exp(lg * (blk - 1.0 - pos))

    s = lax.dot_general(q.astype(BF16), k.astype(BF16), (((1,), (1,)), ((), ())), preferred_element_type=F32)
    s = s * decay_scr[...]
    state = state_scr[...]
    o = jnp.dot(s.astype(BF16), v, preferred_element_type=F32)
    o = o + jnp.dot((q * q_dec).astype(BF16), state.astype(BF16), preferred_element_type=F32)
    kd = (k * k_dec).astype(BF16)
    upd = lax.dot_general(kd, v, (((0,), (0,)), ((), ())), preferred_element_type=F32)
    state_scr[...] = state * jnp.exp(lg * blk) + upd

    o = o * lax.rsqrt(jnp.mean(o * o, axis=-1, keepdims=True) + EPS)
    g = g_ref[...].astype(F32)
    o_ref[...] = (g * _sigmoid(g) * o).astype(BF16)


def _retention(proj, rc, rs, batch, seq):
    t = proj.shape[0]
    blk = min(RET_BLOCK, seq)
    per_b = seq // blk
    log_gamma = jnp.log(1.0 - 2.0 ** (-5.0 - jnp.arange(RET_HEADS, dtype=F32)))
    row = lambda b, h, n: b * per_b + n
    return pl.pallas_call(
        functools.partial(_ret_kernel, blk=blk),
        grid=(batch, RET_HEADS, per_b),
        in_specs=[
            pl.BlockSpec(memory_space=pltpu.SMEM),
            pl.BlockSpec((blk, RET_DK), lambda b, h, n: (row(b, h, n), RQ_OFF // RET_DK + h)),
            pl.BlockSpec((blk, RET_DK), lambda b, h, n: (row(b, h, n), RK_OFF // RET_DK + h)),
            pl.BlockSpec((blk, RET_DV), lambda b, h, n: (row(b, h, n), RV_OFF // RET_DV + h)),
            pl.BlockSpec((blk, RET_DV), lambda b, h, n: (row(b, h, n), RG_OFF // RET_DV + h)),
            pl.BlockSpec((blk, LANES), lambda b, h, n: (row(b, h, n), 0)),
            pl.BlockSpec((blk, LANES), lambda b, h, n: (row(b, h, n), 0)),
        ],
        out_specs=pl.BlockSpec((blk, RET_DV), lambda b, h, n: (row(b, h, n), h)),
        out_shape=jax.ShapeDtypeStruct((t, RET_HEADS * RET_DV), BF16),
        scratch_shapes=[pltpu.VMEM((RET_DK, RET_DV), F32), pltpu.VMEM((blk, blk), F32)],
        compiler_params=_params(("arbitrary", "arbitrary", "arbitrary")),
        name="retention",
    )(log_gamma, proj, proj, proj, proj, rc, rs)


def _mix_kernel(gate_ref, conv_ref, att_ref, ret_ref, x_ref, gt_ref, sh_ref, sc_ref, gf_ref, cw_ref,
                wom_ref, woc_ref, wor_ref, wmo_ref, wrh_ref, wrl_ref, br_ref,
                xo_ref, h_ref, rout_ref, u_scr, *, tm, per_b):
    i = pl.program_id(0)
    d = D_MODEL
    cw = CONV_WIDTH

    cb = conv_ref[:, 0:cw].astype(F32)
    u = conv_ref[:, cw:2 * cw].astype(F32) * conv_ref[:, 2 * cw:3 * cw].astype(F32)

    @pl.when(i % per_b == 0)
    def _():
        u_scr[0:SUBLANES, :] = jnp.zeros((SUBLANES, cw), F32)

    @pl.when(i % per_b != 0)
    def _():
        u_scr[0:SUBLANES, :] = u_scr[tm:tm + SUBLANES, :]

    u_scr[SUBLANES:SUBLANES + tm, :] = u
    y = cw_ref[2:3, :] * u
    y = y + cw_ref[1:2, :] * u_scr[SUBLANES - 1:SUBLANES - 1 + tm, :]
    y = y + cw_ref[0:1, :] * u_scr[SUBLANES - 2:SUBLANES - 2 + tm, :]
    z = (cb * y).astype(BF16)

    y_mla = jnp.dot(att_ref[...], wom_ref[...], preferred_element_type=F32)
    y_conv = jnp.dot(z, woc_ref[...], preferred_element_type=F32)
    y_ret = jnp.dot(ret_ref[...], wor_ref[...], preferred_element_type=F32)
    merged = _sigmoid(gate_ref[:, 0:d].astype(F32)) * y_mla
    merged = merged + _sigmoid(gate_ref[:, d:2 * d].astype(F32)) * y_conv
    merged = merged + _sigmoid(gate_ref[:, 2 * d:3 * d].astype(F32)) * y_ret
    x = x_ref[...] + gt_ref[0] * jnp.dot(merged.astype(BF16), wmo_ref[...], preferred_element_type=F32)
    xo_ref[...] = x

    h = _rms(x, gf_ref[...]) * (1.0 + sc_ref[0]) + sh_ref[0]
    h_ref[...] = h

    h_hi = h.astype(BF16)
    h_lo = (h - h_hi.astype(F32)).astype(BF16)
    lg = jnp.dot(h_hi, wrh_ref[...], preferred_element_type=F32)
    lg = lg + jnp.dot(h_lo, wrh_ref[...], preferred_element_type=F32)
    lg = lg + jnp.dot(h_hi, wrl_ref[...], preferred_element_type=F32)
    lg = lg + br_ref[...]

    lane = lax.broadcasted_iota(jnp.int32, lg.shape, 1)
    neg = -jnp.inf
    gl = jnp.where(lane < N_GROUPS, lg, neg)
    gmax = jnp.max(gl, axis=-1, keepdims=True)
    g_val = 1.0 / jnp.sum(jnp.exp(gl - gmax), axis=-1, keepdims=True)
    g_idx = jnp.min(jnp.where(gl == gmax, lane, LANES), axis=-1, keepdims=True)
    in_group = (lane >= ROUTE_E_OFF) & (lane < ROUTE_E_OFF + N_EXPERTS)
    in_group = in_group & (((lane - ROUTE_E_OFF) // EXPERTS_PER_GROUP) == g_idx)
    el = jnp.where(in_group, lg, neg)
    m1 = jnp.max(el, axis=-1, keepdims=True)
    i1 = jnp.min(jnp.where(el == m1, lane, LANES), axis=-1, keepdims=True)
    el2 = jnp.where(lane == i1, neg, el)
    m2 = jnp.max(el2, axis=-1, keepdims=True)
    i2 = jnp.min(jnp.where(el2 == m2, lane, LANES), axis=-1, keepdims=True)
    r = jnp.exp(m2 - m1)
    w1 = g_val / (1.0 + r)
    w2 = g_val * r / (1.0 + r)
    e1 = (i1 - ROUTE_E_OFF).astype(F32)
    e2 = (i2 - ROUTE_E_OFF).astype(F32)
    rout = jnp.where(lane == 0, e1, jnp.where(lane == 1, e2, jnp.where(lane == 2, w1, jnp.where(lane == 3, w2, 0.0))))
    rout_ref[...] = rout


def _mix(proj, att, ret, x2, mod3, gf, cw, wom, woc, wor, wmo, wrh, wrl, br, seq):
    t, d = x2.shape
    tm = min(ROW_TILE, seq)
    per_b = seq // tm
    row = lambda i: (i, 0)
    modspec = lambda k: pl.BlockSpec((1, 1, d), lambda i: ((i // per_b) * 6 + k, 0, 0))
    return pl.pallas_call(
        functools.partial(_mix_kernel, tm=tm, per_b=per_b),
        grid=(t // tm,),
        in_specs=[
            pl.BlockSpec((tm, GATE_W), row),
            pl.BlockSpec((tm, 3 * CONV_WIDTH), lambda i: (i, CONV_OFF // (3 * CONV_WIDTH))),
            pl.BlockSpec((tm, MLA_HEADS * MLA_V), row),
            pl.BlockSpec((tm, RET_HEADS * RET_DV), row),
            pl.BlockSpec((tm, d), row),
            modspec(2), modspec(3), modspec(4),
            _const_spec((1, d)),
            _const_spec((SUBLANES, CONV_WIDTH)),
            _const_spec(wom.shape), _const_spec(woc.shape), _const_spec(wor.shape), _const_spec(wmo.shape),
            _const_spec(wrh.shape), _const_spec(wrl.shape), _const_spec((1, LANES)),
        ],
        out_specs=[pl.BlockSpec((tm, d), row), pl.BlockSpec((tm, d), row), pl.BlockSpec((tm, LANES), row)],
        out_shape=[
            jax.ShapeDtypeStruct((t, d), F32),
            jax.ShapeDtypeStruct((t, d), F32),
            jax.ShapeDtypeStruct((t, LANES), F32),
        ],
        scratch_shapes=[pltpu.VMEM((tm + SUBLANES, CONV_WIDTH), F32)],
        compiler_params=_params(("arbitrary",)),
        name="mix_router",
    )(proj, proj, att, ret, x2, mod3, mod3, mod3, gf, cw, wom, woc, wor, wmo, wrh, wrl, br)


def _moe_kernel(te_ref, nv_ref, meta_ref, meta_next_ref, wt_ref, wg_ref, wu_ref, wd_ref, h_hbm, out_hbm,
                xbuf, ybuf, wgb, wub, wdb, gsem, ssem, *, tm):
    i = pl.program_id(0)
    nt = pl.num_programs(0)
    slot = i % 2
    nv = nv_ref[i]
    prev = jnp.maximum(i - 1, 0)
    nxt = jnp.minimum(i + 1, nt - 1)

    def gather_copy(tok, s, r):
        return pltpu.make_async_copy(h_hbm.at[pl.ds(tok, 1)], xbuf.at[s, pl.ds(r, 1)], gsem.at[s])

    def scatter_copy(dst, s, r):
        return pltpu.make_async_copy(ybuf.at[s, pl.ds(r, 1)], out_hbm.at[pl.ds(dst, 1)], ssem.at[s])

    def start_gather(meta, s):
        def body(r, carry):
            gather_copy(meta[0, 0, r], s, r).start()
            return carry
        lax.fori_loop(0, tm, body, 0, unroll=8)

    def wait_gather(s):
        def body(r, carry):
            gather_copy(0, s, r).wait()
            return carry
        lax.fori_loop(0, tm, body, 0, unroll=True)

    def start_scatter(s):
        def body(r, carry):
            scatter_copy(meta_ref[0, 0, tm + r], s, r).start()
            return carry
        lax.fori_loop(0, tm, body, 0, unroll=8)

    def wait_scatter(s):
        def body(r, carry):
            scatter_copy(0, s, r).wait()
            return carry
        lax.fori_loop(0, tm, body, 0, unroll=True)

    @pl.when(i == 0)
    def _():
        ybuf[...] = jnp.zeros(ybuf.shape, F32)
        n_pad = 2 * tm
        for s in range(2):
            pad_rows = out_hbm.at[pl.ds(out_hbm.shape[0] - n_pad + s * tm, tm)]
            init = pltpu.make_async_copy(ybuf.at[s], pad_rows, ssem.at[s])
            init.start()
            init.wait()

    @pl.when((i == 0) & (nv > 0))
    def _():
        start_gather(meta_ref, 0)

    @pl.when((i + 1 < nt) & (nv_ref[nxt] > 0))
    def _():
        start_gather(meta_next_ref, 1 - slot)

    @pl.when((i == 0) | (te_ref[i] != te_ref[prev]))
    def _():
        wgb[...] = wg_ref[0].astype(BF16)
        wub[...] = wu_ref[0].astype(BF16)
        wdb[...] = wd_ref[0].astype(BF16)

    @pl.when(nv > 0)
    def _():
        wait_gather(slot)
        x = xbuf[slot].astype(BF16)
        g = jnp.dot(x, wgb[...], preferred_element_type=F32)
        u = jnp.dot(x, wub[...], preferred_element_type=F32)
        hid = (g * _sigmoid(g) * u).astype(BF16)
        ybuf[slot] = jnp.dot(hid, wdb[...], preferred_element_type=F32) * wt_ref[0]

    @pl.when((i > 0) & (nv_ref[prev] > 0))
    def _():
        wait_scatter(1 - slot)

    @pl.when(nv > 0)
    def _():
        start_scatter(slot)

    @pl.when((i == nt - 1) & (nv > 0))
    def _():
        wait_scatter(slot)


def _route_tables(rout, t, tm, nt):
    e = rout[:, 0:2].astype(jnp.int32).reshape(-1)
    w = rout[:, 2:4].reshape(-1)
    n_slot = 2 * t
    perm = jnp.argsort(e).astype(jnp.int32)
    counts = jnp.sum((e[:, None] == jnp.arange(N_EXPERTS, dtype=jnp.int32)[None, :]).astype(jnp.int32), axis=0)
    nt_e = (counts + tm - 1) // tm
    tile_end = jnp.cumsum(nt_e)
    tile_start = tile_end - nt_e
    off = jnp.cumsum(counts) - counts
    total = tile_end[-1]
    i = jnp.arange(nt, dtype=jnp.int32)
    e_raw = jnp.minimum(jnp.searchsorted(tile_end, i, side="right").astype(jnp.int32), N_EXPERTS - 1)
    e_i = jnp.where(i < total, e_raw, e_raw[jnp.maximum(total - 1, 0)])
    j = i - tile_start[e_i]
    nvalid = jnp.where(i < total, jnp.clip(counts[e_i] - j * tm, 0, tm), 0).astype(jnp.int32)
    r = jnp.arange(tm, dtype=jnp.int32)
    pos = jnp.clip((off[e_i] + j * tm)[:, None] + r[None, :], 0, n_slot - 1)
    slot = perm[pos]
    valid = r[None, :] < nvalid[:, None]
    tok = slot // 2
    dst = jnp.where(valid, (slot % 2) * t + tok, n_slot + (i % 2)[:, None] * tm + r[None, :])
    wt = jnp.where(valid, w[slot], 0.0)
    meta = jnp.concatenate([tok, dst], axis=1).astype(jnp.int32).reshape(nt, 1, 2 * tm)
    return e_i, nvalid, meta, wt.reshape(nt, tm, 1)


def _moe(h, rout, wg, wu, wd, layer):
    t, d = h.shape
    tm = min(MOE_TILE, t)
    nt = (2 * t) // tm + N_EXPERTS
    e_i, nvalid, meta, wt = _route_tables(rout, t, tm, nt)
    f = wg.shape[-1]
    grid_spec = pltpu.PrefetchScalarGridSpec(
        num_scalar_prefetch=2,
        grid=(nt,),
        in_specs=[
            pl.BlockSpec((1, 1, 2 * tm), lambda i, te, nv: (i, 0, 0), memory_space=pltpu.SMEM),
            pl.BlockSpec((1, 1, 2 * tm), lambda i, te, nv: (jnp.minimum(i + 1, nt - 1), 0, 0), memory_space=pltpu.SMEM),
            pl.BlockSpec((1, tm, 1), lambda i, te, nv: (i, 0, 0)),
            pl.BlockSpec((1, 1, d, f), lambda i, te, nv: (layer, te[i], 0, 0)),
            pl.BlockSpec((1, 1, d, f), lambda i, te, nv: (layer, te[i], 0, 0)),
            pl.BlockSpec((1, 1, f, d), lambda i, te, nv: (layer, te[i], 0, 0)),
            pl.BlockSpec(memory_space=pl.ANY),
        ],
        out_specs=pl.BlockSpec(memory_space=pl.ANY),
        scratch_shapes=[
            pltpu.VMEM((2, tm, d), F32),
            pltpu.VMEM((2, tm, d), F32),
            pltpu.VMEM((d, f), BF16),
            pltpu.VMEM((d, f), BF16),
            pltpu.VMEM((f, d), BF16),
            pltpu.SemaphoreType.DMA((2,)),
            pltpu.SemaphoreType.DMA((2,)),
        ],
    )

    def kern(te_ref, nv_ref, meta_ref, meta_next_ref, wt_ref, wg_ref, wu_ref, wd_ref, h_hbm, out_hbm, *scratch):
        _moe_kernel(te_ref, nv_ref, meta_ref, meta_next_ref, wt_ref, wg_ref.at[0], wu_ref.at[0], wd_ref.at[0],
                    h_hbm, out_hbm, *scratch, tm=tm)

    return pl.pallas_call(
        kern,
        grid_spec=grid_spec,
        out_shape=jax.ShapeDtypeStruct((2 * t + 2 * tm, d), F32),
        compiler_params=_params(("arbitrary",)),
        name="moe_experts",
    )(e_i, nvalid, meta, meta, wt, wg, wu, wd, h)


def _combine_kernel(x_ref, a_ref, b_ref, gt_ref, g_ref, o_ref, *, final):
    x = x_ref[...] + gt_ref[0] * (a_ref[...] + b_ref[...])
    if final:
        x = _rms(x, g_ref[...])
    o_ref[...] = x


def _combine(x2, moe_out, mod3, final_g, seq, final):
    t, d = x2.shape
    tm = min(512, seq)
    per_b = seq // tm
    return pl.pallas_call(
        functools.partial(_combine_kernel, final=final),
        grid=(t // tm,),
        in_specs=[
            pl.BlockSpec((tm, d), lambda i: (i, 0)),
            pl.BlockSpec((tm, d), lambda i: (i, 0)),
            pl.BlockSpec((tm, d), lambda i: (t // tm + i, 0)),
            pl.BlockSpec((1, 1, d), lambda i: ((i // per_b) * 6 + 5, 0, 0)),
            _const_spec((1, d)),
        ],
        out_specs=pl.BlockSpec((tm, d), lambda i: (i, 0)),
        out_shape=jax.ShapeDtypeStruct((t, d), F32),
        compiler_params=_params(("arbitrary",)),
        name="combine",
    )(x2, moe_out, moe_out, mod3, final_g)


def _prep_w_in(w):
    d = w.shape[0]
    sizes = (MLA_Q_RANK, MLA_KV_RANK, MLA_ROPE, CONV_WIDTH, CONV_WIDTH, CONV_WIDTH,
             RET_HEADS * RET_DK, RET_HEADS * RET_DK, RET_HEADS * RET_DV, RET_HEADS * RET_DV, 3 * D_MODEL)
    parts = []
    start = 0
    for size in sizes:
        parts.append(w[:, start:start + size])
        start += size
    q_lat, kv_lat, kr, cb, cc, cx, rq, rk, rv, rg, gl = parts
    half = MLA_ROPE // 2
    zeros = lambda n: jnp.zeros((d, n), w.dtype)
    kblock = jnp.concatenate(
        [-kr[:, half:], kr[:, :half], zeros(MLA_NOPE - MLA_ROPE), kr, zeros(LANES - MLA_NOPE - MLA_ROPE)], axis=1)
    out = jnp.concatenate([gl, rq, rk, rv, rg, cb, cc, cx, q_lat, kv_lat, kblock], axis=1)
    return out.astype(BF16)


def _prep_w_uq(w):
    r = w.shape[0]
    dq = MLA_NOPE + MLA_ROPE
    half = MLA_ROPE // 2
    w3 = w.reshape(r, MLA_HEADS, dq)
    pad = QPAD - dq
    wa = jnp.pad(w3, ((0, 0), (0, 0), (0, pad))).reshape(r, MLA_HEADS * QPAD)
    wb = jnp.concatenate(
        [jnp.zeros((r, MLA_HEADS, MLA_NOPE), w.dtype), -w3[:, :, MLA_NOPE + half:], w3[:, :, MLA_NOPE:MLA_NOPE + half],
         jnp.zeros((r, MLA_HEADS, pad), w.dtype)], axis=2).reshape(r, MLA_HEADS * QPAD)
    return wa.astype(BF16), wb.astype(BF16)


def _prep_w_ukv(w):
    r = w.shape[0]
    w3 = w.reshape(r, MLA_HEADS, MLA_NOPE + MLA_V)
    wk = jnp.pad(w3[:, :, :MLA_NOPE], ((0, 0), (0, 0), (0, QPAD - MLA_NOPE))).reshape(r, MLA_HEADS * QPAD)
    wv = w3[:, :, MLA_NOPE:].reshape(r, MLA_HEADS * MLA_V)
    return wk.astype(BF16), wv.astype(BF16)


def _prep_router(w_rg, b_rg, w_re, b_re):
    d = w_rg.shape[0]
    w = jnp.zeros((d, LANES), F32)
    w = w.at[:, :N_GROUPS].set(w_rg).at[:, ROUTE_E_OFF:ROUTE_E_OFF + N_EXPERTS].set(w_re)
    b = jnp.zeros((1, LANES), F32)
    b = b.at[0, :N_GROUPS].set(b_rg).at[0, ROUTE_E_OFF:ROUTE_E_OFF + N_EXPERTS].set(b_re)
    w_hi = w.astype(BF16)
    w_lo = (w - w_hi.astype(F32)).astype(BF16)
    return w_hi, w_lo, b


def kernel(x, c, positions, w_ada, b_ada, norm_mix_g, norm_ffn_g, w_in, mla_q_norm_g, mla_kv_norm_g, w_uq, w_ukv, w_o_mla, conv_w, w_o_conv, w_o_ret, w_mix_out, w_route_group, b_route_group, w_route_expert, b_route_expert, w_exp_gate, w_exp_up, w_exp_down, final_g):
    batch, seq, d = x.shape
    depth = w_in.shape[0]
    t = batch * seq
    x2 = x.reshape(t, d)
    mod = _ada_mod(c, w_ada, b_ada)
    rc, rs, mc, ms = _rope_tables(positions)
    for l in range(depth):
        mod3 = mod[l].reshape(batch * 6, 1, d)
        proj = _in_proj(x2, mod3, norm_mix_g[l].reshape(1, d), _prep_w_in(w_in[l]), seq)
        wa, wb = _prep_w_uq(w_uq[l])
        wk, wv = _prep_w_ukv(w_ukv[l])
        q, k, v = _mla_prep(proj, mc, ms, mla_q_norm_g[l].reshape(1, -1), mla_kv_norm_g[l].reshape(1, -1), wa, wb, wk, wv)
        att = _attention(q, k, v, batch, seq)
        ret = _retention(proj, rc, rs, batch, seq)
        wrh, wrl, br = _prep_router(w_route_group[l], b_route_group[l], w_route_expert[l], b_route_expert[l])
        cw = jnp.pad(conv_w[l], ((0, SUBLANES - CONV_K), (0, 0)))
        x2, h, rout = _mix(proj, att, ret, x2, mod3, norm_ffn_g[l].reshape(1, d), cw,
                           w_o_mla[l].astype(BF16), w_o_conv[l].astype(BF16), w_o_ret[l].astype(BF16),
                           w_mix_out[l].astype(BF16), wrh, wrl, br, seq)
        moe_out = _moe(h, rout, w_exp_gate, w_exp_up, w_exp_down, l)
        x2 = _combine(x2, moe_out, mod3, final_g.reshape(1, d), seq, final=(l == depth - 1))
    return x2.reshape(batch, seq, d)
```

```python
import functools

import jax
import jax.numpy as jnp
from jax import lax
from jax.experimental import pallas as pl
from jax.experimental.pallas import tpu as pltpu

F32 = jnp.float32
BF16 = jnp.bfloat16

D_MODEL = 1024
CHUNK = 64
EPS = 1e-6
ROPE_THETA = 10000.0
LOG2E = 1.4426950408889634

MLA_HEADS = 8
MLA_Q_RANK = 384
MLA_KV_RANK = 256
MLA_NOPE = 64
MLA_ROPE = 32
MLA_V = 64

CONV_WIDTH = 512
CONV_K = 3

RET_HEADS = 4
RET_DK = 128
RET_DV = 256

N_GROUPS = 4
EXPERTS_PER_GROUP = 8
N_EXPERTS = N_GROUPS * EXPERTS_PER_GROUP
EXPERT_FF = 512

LANES = 128
SUBLANES = 8
VMEM_LIMIT = 52 * 1024 * 1024

GATE_W = 3 * D_MODEL
RQ_OFF = GATE_W
RK_OFF = RQ_OFF + RET_HEADS * RET_DK
RV_OFF = RK_OFF + RET_HEADS * RET_DK
RG_OFF = RV_OFF + RET_HEADS * RET_DV
CONV_OFF = RG_OFF + RET_HEADS * RET_DV
MLA_OFF = CONV_OFF + 3 * CONV_WIDTH
MLA_W = MLA_Q_RANK + MLA_KV_RANK + LANES
PROJ_W = MLA_OFF + MLA_W
PROJ_TN = PROJ_W // 6
assert PROJ_TN * 6 == PROJ_W and PROJ_TN % LANES == 0
assert MLA_OFF % MLA_W == 0 and CONV_OFF % (3 * CONV_WIDTH) == 0

QPAD = LANES
ROUTE_E_OFF = 32

ROW_TILE_PROJ = 1024
ROW_TILE = 256
ATT_TILE = 512
RET_BLOCK = 256
MOE_TILE = 256
ADA_TN = 768


def _params(sem, vmem=VMEM_LIMIT):
    return pltpu.CompilerParams(dimension_semantics=sem, vmem_limit_bytes=vmem)


def _const_spec(shape):
    nd = len(shape)
    return pl.BlockSpec(shape, lambda *_: (0,) * nd)


def _rms(x, g):
    return x * lax.rsqrt(jnp.mean(x * x, axis=-1, keepdims=True) + EPS) * g


def _sigmoid(x):
    return 1.0 / (1.0 + jnp.exp(-x))


def _ada_kernel(ct_ref, w_ref, b_ref, o_ref):
    ct = ct_ref[...]
    a = ct * _sigmoid(ct)
    w = w_ref[0]
    rows = [jnp.sum(w * a[:, b:b + 1], axis=0, keepdims=True) for b in range(ct.shape[1])]
    o_ref[0] = jnp.concatenate(rows, axis=0) + b_ref[0]


def _ada_mod(c, w_ada, b_ada):
    depth, d, n = w_ada.shape
    b = c.shape[0]
    return pl.pallas_call(
        _ada_kernel,
        grid=(depth, n // ADA_TN),
        in_specs=[
            _const_spec((d, b)),
            pl.BlockSpec((1, d, ADA_TN), lambda l, j: (l, 0, j)),
            pl.BlockSpec((1, 1, ADA_TN), lambda l, j: (l, 0, j)),
        ],
        out_specs=pl.BlockSpec((1, b, ADA_TN), lambda l, j: (l, 0, j)),
        out_shape=jax.ShapeDtypeStruct((depth, b, n), F32),
        compiler_params=_params(("arbitrary", "arbitrary")),
        name="ada_mod",
    )(c.T, w_ada, b_ada.reshape(depth, 1, n))


def _rope_kernel(pos_ref, f_ref, rc_ref, rs_ref, mc_ref, ms_ref):
    ang = pos_ref[...].astype(F32) * f_ref[...]
    cs = jnp.cos(ang)
    sn = jnp.sin(ang)
    lane = lax.broadcasted_iota(jnp.int32, ang.shape, 1)
    half = LANES // 2
    lo = lane < half
    rc_ref[...] = jnp.where(lo, cs, pltpu.roll(cs, half, 1))
    rs_ref[...] = jnp.where(lo, -sn, pltpu.roll(sn, half, 1))
    rope = (lane >= MLA_NOPE) & (lane < MLA_NOPE + MLA_ROPE)
    mc_ref[...] = jnp.where(lane < MLA_NOPE, 1.0, jnp.where(rope, cs, 0.0))
    ms_ref[...] = jnp.where(rope, sn, 0.0)


def _rope_tables(positions):
    t = positions.size
    tm = min(2048, t)
    inv_ret = ROPE_THETA ** (-jnp.arange(0, RET_DK, 2, dtype=F32) / RET_DK)
    inv_mla = ROPE_THETA ** (-jnp.arange(0, MLA_ROPE, 2, dtype=F32) / MLA_ROPE)
    freqs = jnp.concatenate([inv_ret, inv_mla, inv_mla, jnp.zeros((LANES - 96,), F32)]).reshape(1, LANES)
    spec = pl.BlockSpec((tm, LANES), lambda i: (i, 0))
    return pl.pallas_call(
        _rope_kernel,
        grid=(t // tm,),
        in_specs=[pl.BlockSpec((tm, 1), lambda i: (i, 0)), _const_spec((1, LANES))],
        out_specs=[spec] * 4,
        out_shape=[jax.ShapeDtypeStruct((t, LANES), F32)] * 4,
        compiler_params=_params(("arbitrary",)),
        name="rope_tables",
    )(positions.reshape(t, 1), freqs)


def _proj_kernel(x_ref, sh_ref, sc_ref, g_ref, w_ref, o_ref, h_scr):
    @pl.when(pl.program_id(1) == 0)
    def _():
        h = _rms(x_ref[...], g_ref[...]) * (1.0 + sc_ref[0]) + sh_ref[0]
        h_scr[...] = h.astype(BF16)

    o_ref[...] = jnp.dot(h_scr[...], w_ref[...], preferred_element_type=F32).astype(BF16)


def _in_proj(x2, mod3, g, w, seq):
    t, d = x2.shape
    tm = min(ROW_TILE_PROJ, seq)
    per_b = seq // tm
    return pl.pallas_call(
        _proj_kernel,
        grid=(t // tm, PROJ_W // PROJ_TN),
        in_specs=[
            pl.BlockSpec((tm, d), lambda i, j: (i, 0)),
            pl.BlockSpec((1, 1, d), lambda i, j: ((i // per_b) * 6 + 0, 0, 0)),
            pl.BlockSpec((1, 1, d), lambda i, j: ((i // per_b) * 6 + 1, 0, 0)),
            _const_spec((1, d)),
            pl.BlockSpec((d, PROJ_TN), lambda i, j: (0, j)),
        ],
        out_specs=pl.BlockSpec((tm, PROJ_TN), lambda i, j: (i, j)),
        out_shape=jax.ShapeDtypeStruct((t, PROJ_W), BF16),
        scratch_shapes=[pltpu.VMEM((tm, d), BF16)],
        compiler_params=_params(("arbitrary", "arbitrary")),
        name="in_proj",
    )(x2, mod3, mod3, g, w)


def _mla_prep_kernel(in_ref, mc_ref, ms_ref, gq_ref, gkv_ref, wa_ref, wb_ref, wk_ref, wv_ref, q_ref, k_ref, v_ref):
    blk = in_ref[...].astype(F32)
    qn = _rms(blk[:, :MLA_Q_RANK], gq_ref[...]).astype(BF16)
    kvn = _rms(blk[:, MLA_Q_RANK:MLA_Q_RANK + MLA_KV_RANK], gkv_ref[...]).astype(BF16)
    kb = blk[:, MLA_Q_RANK + MLA_KV_RANK:]
    mc = mc_ref[...]
    ms = ms_ref[...]
    qa = jnp.dot(qn, wa_ref[...], preferred_element_type=F32)
    qb = jnp.dot(qn, wb_ref[...], preferred_element_type=F32)
    kn = jnp.dot(kvn, wk_ref[...], preferred_element_type=F32)
    lane = lax.broadcasted_iota(jnp.int32, kb.shape, 1)
    rope = (lane >= MLA_NOPE) & (lane < MLA_NOPE + MLA_ROPE)
    kpe = jnp.where(rope, kb * mc + pltpu.roll(kb, LANES // 2, 1) * ms, 0.0)
    scale = (MLA_NOPE + MLA_ROPE) ** -0.5 * LOG2E
    for h in range(MLA_HEADS):
        sl = slice(h * QPAD, (h + 1) * QPAD)
        q_ref[:, sl] = ((qa[:, sl] * mc + qb[:, sl] * ms) * scale).astype(BF16)
        k_ref[:, sl] = (kn[:, sl] + kpe).astype(BF16)
    vt = lax.dot_general(wv_ref[...], kvn, (((1,), (1,)), ((), ())), preferred_element_type=F32)
    row = lax.broadcasted_iota(jnp.int32, vt.shape, 0)
    v_ref[0, 0] = jnp.where(row % QPAD >= MLA_V, 1.0, vt).astype(BF16)


def _mla_prep(proj, mc, ms, gq, gkv, wa, wb, wk, wv, batch, seq):
    t = proj.shape[0]
    tm = min(ATT_TILE, seq)
    per_b = seq // tm
    hq = MLA_HEADS * QPAD
    hv = MLA_HEADS * QPAD
    return pl.pallas_call(
        _mla_prep_kernel,
        grid=(t // tm,),
        in_specs=[
            pl.BlockSpec((tm, MLA_W), lambda i: (i, MLA_OFF // MLA_W)),
            pl.BlockSpec((tm, LANES), lambda i: (i, 0)),
            pl.BlockSpec((tm, LANES), lambda i: (i, 0)),
            _const_spec((1, MLA_Q_RANK)),
            _const_spec((1, MLA_KV_RANK)),
            _const_spec((MLA_Q_RANK, hq)),
            _const_spec((MLA_Q_RANK, hq)),
            _const_spec((MLA_KV_RANK, hq)),
            _const_spec((hv, MLA_KV_RANK)),
        ],
        out_specs=[
            pl.BlockSpec((tm, hq), lambda i: (i, 0)),
            pl.BlockSpec((tm, hq), lambda i: (i, 0)),
            pl.BlockSpec((1, 1, hv, tm), lambda i: (i // per_b, i % per_b, 0, 0)),
        ],
        out_shape=[
            jax.ShapeDtypeStruct((t, hq), BF16),
            jax.ShapeDtypeStruct((t, hq), BF16),
            jax.ShapeDtypeStruct((batch, per_b, hv, tm), BF16),
        ],
        compiler_params=_params(("arbitrary",)),
        name="mla_prep",
    )(proj, mc, ms, gq, gkv, wa, wb, wk, wv)


def _attn_kernel(q_ref, k_ref, vt_ref, o_ref, m_scr, acc_scr, *, tile):
    qi = pl.program_id(2)
    key = lax.broadcasted_iota(jnp.int32, (tile, tile), 0)
    qry = lax.broadcasted_iota(jnp.int32, (tile, tile), 1)
    diag_mask = (key // CHUNK) <= (qry // CHUNK)

    m_scr[...] = jnp.full(m_scr.shape, -jnp.inf, F32)
    acc_scr[...] = jnp.zeros(acc_scr.shape, F32)

    def step(j, masked):
        start = pl.multiple_of(j * tile, tile)
        for a in range(2):
            q = q_ref[0, :, a * QPAD:(a + 1) * QPAD]
            k = k_ref[0, pl.ds(start, tile), a * QPAD:(a + 1) * QPAD]
            st = lax.dot_general(k, q, (((1,), (1,)), ((), ())), preferred_element_type=F32)
            if masked:
                st = jnp.where(diag_mask, st, -jnp.inf)
            m_old = m_scr[a]
            m_new = jnp.maximum(m_old, jnp.max(st, axis=0, keepdims=True))
            p = jnp.exp2(st - m_new).astype(BF16)
            alpha = jnp.exp2(m_old - m_new)
            vt = vt_ref[0, j, a * QPAD:(a + 1) * QPAD, :]
            acc_scr[a] = alpha * acc_scr[a] + jnp.dot(vt, p, preferred_element_type=F32)
            m_scr[a] = m_new

    def body(j, carry):
        step(j, False)
        return carry

    lax.fori_loop(0, qi, body, 0)
    step(qi, True)

    lane = lax.broadcasted_iota(jnp.int32, (tile, LANES), 1)
    r0 = acc_scr[0].T
    r1 = acc_scr[1].T
    o0 = r0 / r0[:, MLA_V:MLA_V + 1]
    o1 = pltpu.roll(r1, MLA_V, 1) / r1[:, MLA_V:MLA_V + 1]
    o_ref[0] = jnp.where(lane < MLA_V, o0, o1).astype(BF16)


def _attention(q, k, vt, batch, seq):
    tile = min(ATT_TILE, seq)
    q3 = q.reshape(batch, seq, MLA_HEADS * QPAD)
    k3 = k.reshape(batch, seq, MLA_HEADS * QPAD)
    out = pl.pallas_call(
        functools.partial(_attn_kernel, tile=tile),
        grid=(batch, MLA_HEADS // 2, seq // tile),
        in_specs=[
            pl.BlockSpec((1, tile, 2 * QPAD), lambda b, h, i: (b, i, h)),
            pl.BlockSpec((1, seq, 2 * QPAD), lambda b, h, i: (b, 0, h)),
            pl.BlockSpec((1, seq // tile, 2 * QPAD, tile), lambda b, h, i: (b, 0, h, 0)),
        ],
        out_specs=pl.BlockSpec((1, tile, 2 * MLA_V), lambda b, h, i: (b, i, h)),
        out_shape=jax.ShapeDtypeStruct((batch, seq, MLA_HEADS * MLA_V), BF16),
        scratch_shapes=[
            pltpu.VMEM((2, 1, tile), F32),
            pltpu.VMEM((2, QPAD, tile), F32),
        ],
        compiler_params=_params(("arbitrary", "arbitrary", "arbitrary")),
        name="mla_attention",
    )(q3, k3, vt)
    return out.reshape(batch * seq, MLA_HEADS * MLA_V)


def _ret_kernel(lg_ref, q_ref, k_ref, v_ref, g_ref, rc_ref, rs_ref, o_ref, state_scr, decay_scr, *, blk):
    h = pl.program_id(1)
    n = pl.program_id(2)
    lg = lg_ref[h]

    @pl.when(n == 0)
    def _():
        state_scr[...] = jnp.zeros(state_scr.shape, F32)
        row = lax.broadcasted_iota(jnp.int32, (blk, blk), 0)
        col = lax.broadcasted_iota(jnp.int32, (blk, blk), 1)
        dist = jnp.abs(row - col).astype(F32)
        decay_scr[...] = jnp.where((col // CHUNK) <= (row // CHUNK), jnp.exp(lg * dist), 0.0)

    rc = rc_ref[...]
    rs = rs_ref[...]
    half = RET_DK // 2
    qf = q_ref[...].astype(F32)
    kf = k_ref[...].astype(F32)
    q = qf * rc + pltpu.roll(qf, half, 1) * rs
    k = (kf * rc + pltpu.roll(kf, half, 1) * rs) * (RET_DK ** -0.5)
    v = v_ref[...]

    pos = lax.broadcasted_iota(jnp.int32, (blk, 1), 0).astype(F32)
    q_dec = jnp.exp(lg * (pos + 1.0))
    k_dec = jnp.exp(lg * (blk - 1.0 - pos))

    s = lax.dot_general(q.astype(BF16), k.astype(BF16), (((1,), (1,)), ((), ())), preferred_element_type=F32)
    s = s * decay_scr[...]
    state = state_scr[...]
    o = jnp.dot(s.astype(BF16), v, preferred_element_type=F32)
    o = o + jnp.dot((q * q_dec).astype(BF16), state.astype(BF16), preferred_element_type=F32)
    kd = (k * k_dec).astype(BF16)
    upd = lax.dot_general(kd, v, (((0,), (0,)), ((), ())), preferred_element_type=F32)
    state_scr[...] = state * jnp.exp(lg * blk) + upd

    o = o * lax.rsqrt(jnp.mean(o * o, axis=-1, keepdims=True) + EPS)
    g = g_ref[...].astype(F32)
    o_ref[...] = (g * _sigmoid(g) * o).astype(BF16)


def _retention(proj, rc, rs, batch, seq):
    t = proj.shape[0]
    blk = min(RET_BLOCK, seq)
    per_b = seq // blk
    log_gamma = jnp.log(1.0 - 2.0 ** (-5.0 - jnp.arange(RET_HEADS, dtype=F32)))
    row = lambda b, h, n: b * per_b + n
    return pl.pallas_call(
        functools.partial(_ret_kernel, blk=blk),
        grid=(batch, RET_HEADS, per_b),
        in_specs=[
            pl.BlockSpec(memory_space=pltpu.SMEM),
            pl.BlockSpec((blk, RET_DK), lambda b, h, n: (row(b, h, n), RQ_OFF // RET_DK + h)),
            pl.BlockSpec((blk, RET_DK), lambda b, h, n: (row(b, h, n), RK_OFF // RET_DK + h)),
            pl.BlockSpec((blk, RET_DV), lambda b, h, n: (row(b, h, n), RV_OFF // RET_DV + h)),
            pl.BlockSpec((blk, RET_DV), lambda b, h, n: (row(b, h, n), RG_OFF // RET_DV + h)),
            pl.BlockSpec((blk, LANES), lambda b, h, n: (row(b, h, n), 0)),
            pl.BlockSpec((blk, LANES), lambda b, h, n: (row(b, h, n), 0)),
        ],
        out_specs=pl.BlockSpec((blk, RET_DV), lambda b, h, n: (row(b, h, n), h)),
        out_shape=jax.ShapeDtypeStruct((t, RET_HEADS * RET_DV), BF16),
        scratch_shapes=[pltpu.VMEM((RET_DK, RET_DV), F32), pltpu.VMEM((blk, blk), F32)],
        compiler_params=_params(("arbitrary", "arbitrary", "arbitrary")),
        name="retention",
    )(log_gamma, proj, proj, proj, proj, rc, rs)


def _mix_kernel(gate_ref, conv_ref, att_ref, ret_ref, x_ref, gt_ref, sh_ref, sc_ref, gf_ref, cw_ref,
                wom_ref, woc_ref, wor_ref, wmo_ref, wrh_ref, wrl_ref, br_ref,
                xo_ref, h_ref, rout_ref, u_scr, *, tm, per_b):
    i = pl.program_id(0)
    d = D_MODEL
    cw = CONV_WIDTH

    cb = conv_ref[:, 0:cw].astype(F32)
    u = conv_ref[:, cw:2 * cw].astype(F32) * conv_ref[:, 2 * cw:3 * cw].astype(F32)

    @pl.when(i % per_b == 0)
    def _():
        u_scr[0:SUBLANES, :] = jnp.zeros((SUBLANES, cw), F32)

    @pl.when(i % per_b != 0)
    def _():
        u_scr[0:SUBLANES, :] = u_scr[tm:tm + SUBLANES, :]

    u_scr[SUBLANES:SUBLANES + tm, :] = u
    y = cw_ref[2:3, :] * u
    y = y + cw_ref[1:2, :] * u_scr[SUBLANES - 1:SUBLANES - 1 + tm, :]
    y = y + cw_ref[0:1, :] * u_scr[SUBLANES - 2:SUBLANES - 2 + tm, :]
    z = (cb * y).astype(BF16)

    y_mla = jnp.dot(att_ref[...], wom_ref[...], preferred_element_type=F32)
    y_conv = jnp.dot(z, woc_ref[...], preferred_element_type=F32)
    y_ret = jnp.dot(ret_ref[...], wor_ref[...], preferred_element_type=F32)
    merged = _sigmoid(gate_ref[:, 0:d].astype(F32)) * y_mla
    merged = merged + _sigmoid(gate_ref[:, d:2 * d].astype(F32)) * y_conv
    merged = merged + _sigmoid(gate_ref[:, 2 * d:3 * d].astype(F32)) * y_ret
    x = x_ref[...] + gt_ref[0] * jnp.dot(merged.astype(BF16), wmo_ref[...], preferred_element_type=F32)
    xo_ref[...] = x

    h = _rms(x, gf_ref[...]) * (1.0 + sc_ref[0]) + sh_ref[0]
    h_ref[...] = h

    h_hi = h.astype(BF16)
    h_lo = (h - h_hi.astype(F32)).astype(BF16)
    lg = jnp.dot(h_hi, wrh_ref[...], preferred_element_type=F32)
    lg = lg + jnp.dot(h_lo, wrh_ref[...], preferred_element_type=F32)
    lg = lg + jnp.dot(h_hi, wrl_ref[...], preferred_element_type=F32)
    lg = lg + br_ref[...]

    lane = lax.broadcasted_iota(jnp.int32, lg.shape, 1)
    neg = -jnp.inf
    gl = jnp.where(lane < N_GROUPS, lg, neg)
    gmax = jnp.max(gl, axis=-1, keepdims=True)
    g_val = 1.0 / jnp.sum(jnp.exp(gl - gmax), axis=-1, keepdims=True)
    g_idx = jnp.min(jnp.where(gl == gmax, lane, LANES), axis=-1, keepdims=True)
    in_group = (lane >= ROUTE_E_OFF) & (lane < ROUTE_E_OFF + N_EXPERTS)
    in_group = in_group & (((lane - ROUTE_E_OFF) // EXPERTS_PER_GROUP) == g_idx)
    el = jnp.where(in_group, lg, neg)
    m1 = jnp.max(el, axis=-1, keepdims=True)
    i1 = jnp.min(jnp.where(el == m1, lane, LANES), axis=-1, keepdims=True)
    el2 = jnp.where(lane == i1, neg, el)
    m2 = jnp.max(el2, axis=-1, keepdims=True)
    i2 = jnp.min(jnp.where(el2 == m2, lane, LANES), axis=-1, keepdims=True)
    r = jnp.exp(m2 - m1)
    w1 = g_val / (1.0 + r)
    w2 = g_val * r / (1.0 + r)
    e1 = (i1 - ROUTE_E_OFF).astype(F32)
    e2 = (i2 - ROUTE_E_OFF).astype(F32)
    rout = jnp.where(lane == 0, e1, jnp.where(lane == 1, e2, jnp.where(lane == 2, w1, jnp.where(lane == 3, w2, 0.0))))
    rout_ref[...] = rout


def _mix(proj, att, ret, x2, mod3, gf, cw, wom, woc, wor, wmo, wrh, wrl, br, seq):
    t, d = x2.shape
    tm = min(ROW_TILE, seq)
    per_b = seq // tm
    row = lambda i: (i, 0)
    modspec = lambda k: pl.BlockSpec((1, 1, d), lambda i: ((i // per_b) * 6 + k, 0, 0))
    return pl.pallas_call(
        functools.partial(_mix_kernel, tm=tm, per_b=per_b),
        grid=(t // tm,),
        in_specs=[
            pl.BlockSpec((tm, GATE_W), row),
            pl.BlockSpec((tm, 3 * CONV_WIDTH), lambda i: (i, CONV_OFF // (3 * CONV_WIDTH))),
            pl.BlockSpec((tm, MLA_HEADS * MLA_V), row),
            pl.BlockSpec((tm, RET_HEADS * RET_DV), row),
            pl.BlockSpec((tm, d), row),
            modspec(2), modspec(3), modspec(4),
            _const_spec((1, d)),
            _const_spec((SUBLANES, CONV_WIDTH)),
            _const_spec(wom.shape), _const_spec(woc.shape), _const_spec(wor.shape), _const_spec(wmo.shape),
            _const_spec(wrh.shape), _const_spec(wrl.shape), _const_spec((1, LANES)),
        ],
        out_specs=[pl.BlockSpec((tm, d), row), pl.BlockSpec((tm, d), row), pl.BlockSpec((tm, LANES), row)],
        out_shape=[
            jax.ShapeDtypeStruct((t, d), F32),
            jax.ShapeDtypeStruct((t, d), F32),
            jax.ShapeDtypeStruct((t, LANES), F32),
        ],
        scratch_shapes=[pltpu.VMEM((tm + SUBLANES, CONV_WIDTH), F32)],
        compiler_params=_params(("arbitrary",)),
        name="mix_router",
    )(proj, proj, att, ret, x2, mod3, mod3, mod3, gf, cw, wom, woc, wor, wmo, wrh, wrl, br)


def _moe_kernel(te_ref, nv_ref, meta_ref, meta_next_ref, wt_ref, wg_ref, wu_ref, wd_ref, h_hbm, out_hbm,
                xbuf, ybuf, wgb, wub, wdb, gsem, ssem, *, tm):
    i = pl.program_id(0)
    nt = pl.num_programs(0)
    slot = i % 2
    nv = nv_ref[i]
    prev = jnp.maximum(i - 1, 0)
    prev2 = jnp.maximum(i - 2, 0)

    def gather_copy(tok, s, r):
        return pltpu.make_async_copy(h_hbm.at[pl.ds(tok, 1)], xbuf.at[s, pl.ds(r, 1)], gsem.at[s])

    def scatter_copy(dst, s, r):
        return pltpu.make_async_copy(ybuf.at[s, pl.ds(r, 1)], out_hbm.at[pl.ds(dst, 1)], ssem.at[s])

    def start_gather(meta, s):
        for r in range(tm):
            gather_copy(meta[0, 0, r], s, r).start()

    def wait_gather(s):
        for r in range(tm):
            gather_copy(0, s, r).wait()

    def start_scatter(s):
        for r in range(tm):
            scatter_copy(meta_ref[0, 0, tm + r], s, r).start()

    def wait_scatter(s):
        for r in range(tm):
            scatter_copy(0, s, r).wait()

    def for_slot(fn):
        @pl.when(slot == 0)
        def _():
            fn(0)

        @pl.when(slot == 1)
        def _():
            fn(1)

    @pl.when(i == 0)
    def _():
        ybuf[...] = jnp.zeros(ybuf.shape, F32)
        n_pad = 2 * tm
        for s in range(2):
            pad_rows = out_hbm.at[pl.ds(out_hbm.shape[0] - n_pad + s * tm, tm)]
            init = pltpu.make_async_copy(ybuf.at[s], pad_rows, ssem.at[s])
            init.start()
            init.wait()
        start_gather(meta_ref, 0)

    @pl.when((i >= 2) & (nv_ref[prev2] > 0))
    def _():
        for_slot(wait_scatter)

    @pl.when((i == 0) | (te_ref[i] != te_ref[prev]))
    def _():
        wgb[...] = wg_ref[0].astype(BF16)
        wub[...] = wu_ref[0].astype(BF16)
        wdb[...] = wd_ref[0].astype(BF16)

    def active(s):
        wait_gather(s)
        x = xbuf[s].astype(BF16)
        start_gather(meta_next_ref, 1 - s)
        g = jnp.dot(x, wgb[...], preferred_element_type=F32)
        u = jnp.dot(x, wub[...], preferred_element_type=F32)
        hid = (g * _sigmoid(g) * u).astype(BF16)
        ybuf[s] = jnp.dot(hid, wdb[...], preferred_element_type=F32) * wt_ref[0]
        start_scatter(s)

    @pl.when(nv > 0)
    def _():
        for_slot(active)

    @pl.when((i > 0) & (nv == 0) & (nv_ref[prev] > 0))
    def _():
        for_slot(wait_gather)

    @pl.when((i == nt - 1) & (nv_ref[prev] > 0))
    def _():
        for_slot(lambda s: wait_scatter(1 - s))


def _route_tables(rout, t, tm, nt):
    e = rout[:, 0:2].astype(jnp.int32).reshape(-1)
    w = rout[:, 2:4].reshape(-1)
    n_slot = 2 * t
    perm = jnp.argsort(e).astype(jnp.int32)
    counts = jnp.sum((e[:, None] == jnp.arange(N_EXPERTS, dtype=jnp.int32)[None, :]).astype(jnp.int32), axis=0)
    nt_e = (counts + tm - 1) // tm
    tile_end = jnp.cumsum(nt_e)
    tile_start = tile_end - nt_e
    off = jnp.cumsum(counts) - counts
    total = tile_end[-1]
    i = jnp.arange(nt, dtype=jnp.int32)
    e_raw = jnp.minimum(jnp.searchsorted(tile_end, i, side="right").astype(jnp.int32), N_EXPERTS - 1)
    e_i = jnp.where(i < total, e_raw, e_raw[jnp.maximum(total - 1, 0)])
    j = i - tile_start[e_i]
    nvalid = jnp.where(i < total, jnp.clip(counts[e_i] - j * tm, 0, tm), 0).astype(jnp.int32)
    r = jnp.arange(tm, dtype=jnp.int32)
    pos = jnp.clip((off[e_i] + j * tm)[:, None] + r[None, :], 0, n_slot - 1)
    slot = perm[pos]
    valid = r[None, :] < nvalid[:, None]
    tok = slot // 2
    dst = jnp.where(valid, (slot % 2) * t + tok, n_slot + (i % 2)[:, None] * tm + r[None, :])
    wt = jnp.where(valid, w[slot], 0.0)
    meta = jnp.concatenate([tok, dst], axis=1).astype(jnp.int32).reshape(nt, 1, 2 * tm)
    return e_i, nvalid, meta, wt.reshape(nt, tm, 1)


def _moe(h, rout, wg, wu, wd, layer):
    t, d = h.shape
    tm = min(MOE_TILE, t)
    nt = (2 * t) // tm + N_EXPERTS
    e_i, nvalid, meta, wt = _route_tables(rout, t, tm, nt)
    f = wg.shape[-1]
    grid_spec = pltpu.PrefetchScalarGridSpec(
        num_scalar_prefetch=2,
        grid=(nt,),
        in_specs=[
            pl.BlockSpec((1, 1, 2 * tm), lambda i, te, nv: (i, 0, 0), memory_space=pltpu.SMEM),
            pl.BlockSpec((1, 1, 2 * tm), lambda i, te, nv: (jnp.minimum(i + 1, nt - 1), 0, 0), memory_space=pltpu.SMEM),
            pl.BlockSpec((1, tm, 1), lambda i, te, nv: (i, 0, 0)),
            pl.BlockSpec((1, 1, d, f), lambda i, te, nv: (layer, te[i], 0, 0)),
            pl.BlockSpec((1, 1, d, f), lambda i, te, nv: (layer, te[i], 0, 0)),
            pl.BlockSpec((1, 1, f, d), lambda i, te, nv: (layer, te[i], 0, 0)),
            pl.BlockSpec(memory_space=pl.ANY),
        ],
        out_specs=pl.BlockSpec(memory_space=pl.ANY),
        scratch_shapes=[
            pltpu.VMEM((2, tm, d), F32),
            pltpu.VMEM((2, tm, d), F32),
            pltpu.VMEM((d, f), BF16),
            pltpu.VMEM((d, f), BF16),
            pltpu.VMEM((f, d), BF16),
            pltpu.SemaphoreType.DMA((2,)),
            pltpu.SemaphoreType.DMA((2,)),
        ],
    )

    def kern(te_ref, nv_ref, meta_ref, meta_next_ref, wt_ref, wg_ref, wu_ref, wd_ref, h_hbm, out_hbm, *scratch):
        _moe_kernel(te_ref, nv_ref, meta_ref, meta_next_ref, wt_ref, wg_ref.at[0], wu_ref.at[0], wd_ref.at[0],
                    h_hbm, out_hbm, *scratch, tm=tm)

    return pl.pallas_call(
        kern,
        grid_spec=grid_spec,
        out_shape=jax.ShapeDtypeStruct((2 * t + 2 * tm, d), F32),
        compiler_params=_params(("arbitrary",)),
        name="moe_experts",
    )(e_i, nvalid, meta, meta, wt, wg, wu, wd, h)


def _combine_kernel(x_ref, a_ref, b_ref, gt_ref, g_ref, o_ref, *, final):
    x = x_ref[...] + gt_ref[0] * (a_ref[...] + b_ref[...])
    if final:
        x = _rms(x, g_ref[...])
    o_ref[...] = x


def _combine(x2, moe_out, mod3, final_g, seq, final):
    t, d = x2.shape
    tm = min(512, seq)
    per_b = seq // tm
    return pl.pallas_call(
        functools.partial(_combine_kernel, final=final),
        grid=(t // tm,),
        in_specs=[
            pl.BlockSpec((tm, d), lambda i: (i, 0)),
            pl.BlockSpec((tm, d), lambda i: (i, 0)),
            pl.BlockSpec((tm, d), lambda i: (t // tm + i, 0)),
            pl.BlockSpec((1, 1, d), lambda i: ((i // per_b) * 6 + 5, 0, 0)),
            _const_spec((1, d)),
        ],
        out_specs=pl.BlockSpec((tm, d), lambda i: (i, 0)),
        out_shape=jax.ShapeDtypeStruct((t, d), F32),
        compiler_params=_params(("arbitrary",)),
        name="combine",
    )(x2, moe_out, moe_out, mod3, final_g)


def _prep_w_in(w):
    d = w.shape[0]
    sizes = (MLA_Q_RANK, MLA_KV_RANK, MLA_ROPE, CONV_WIDTH, CONV_WIDTH, CONV_WIDTH,
             RET_HEADS * RET_DK, RET_HEADS * RET_DK, RET_HEADS * RET_DV, RET_HEADS * RET_DV, 3 * D_MODEL)
    parts = []
    start = 0
    for size in sizes:
        parts.append(w[:, start:start + size])
        start += size
    q_lat, kv_lat, kr, cb, cc, cx, rq, rk, rv, rg, gl = parts
    half = MLA_ROPE // 2
    zeros = lambda n: jnp.zeros((d, n), w.dtype)
    kblock = jnp.concatenate(
        [-kr[:, half:], kr[:, :half], zeros(MLA_NOPE - MLA_ROPE), kr, zeros(LANES - MLA_NOPE - MLA_ROPE)], axis=1)
    out = jnp.concatenate([gl, rq, rk, rv, rg, cb, cc, cx, q_lat, kv_lat, kblock], axis=1)
    return out.astype(BF16)


def _prep_w_uq(w):
    r = w.shape[0]
    dq = MLA_NOPE + MLA_ROPE
    half = MLA_ROPE // 2
    w3 = w.reshape(r, MLA_HEADS, dq)
    pad = QPAD - dq
    wa = jnp.pad(w3, ((0, 0), (0, 0), (0, pad))).reshape(r, MLA_HEADS * QPAD)
    wb = jnp.concatenate(
        [jnp.zeros((r, MLA_HEADS, MLA_NOPE), w.dtype), -w3[:, :, MLA_NOPE + half:], w3[:, :, MLA_NOPE:MLA_NOPE + half],
         jnp.zeros((r, MLA_HEADS, pad), w.dtype)], axis=2).reshape(r, MLA_HEADS * QPAD)
    return wa.astype(BF16), wb.astype(BF16)


def _prep_w_ukv(w):
    r = w.shape[0]
    w3 = w.reshape(r, MLA_HEADS, MLA_NOPE + MLA_V)
    wk = jnp.pad(w3[:, :, :MLA_NOPE], ((0, 0), (0, 0), (0, QPAD - MLA_NOPE))).reshape(r, MLA_HEADS * QPAD)
    wv = jnp.pad(w3[:, :, MLA_NOPE:], ((0, 0), (0, 0), (0, QPAD - MLA_V))).reshape(r, MLA_HEADS * QPAD)
    return wk.astype(BF16), wv.T.astype(BF16)


def _prep_router(w_rg, b_rg, w_re, b_re):
    d = w_rg.shape[0]
    w = jnp.zeros((d, LANES), F32)
    w = w.at[:, :N_GROUPS].set(w_rg).at[:, ROUTE_E_OFF:ROUTE_E_OFF + N_EXPERTS].set(w_re)
    b = jnp.zeros((1, LANES), F32)
    b = b.at[0, :N_GROUPS].set(b_rg).at[0, ROUTE_E_OFF:ROUTE_E_OFF + N_EXPERTS].set(b_re)
    w_hi = w.astype(BF16)
    w_lo = (w - w_hi.astype(F32)).astype(BF16)
    return w_hi, w_lo, b


def kernel(x, c, positions, w_ada, b_ada, norm_mix_g, norm_ffn_g, w_in, mla_q_norm_g, mla_kv_norm_g, w_uq, w_ukv, w_o_mla, conv_w, w_o_conv, w_o_ret, w_mix_out, w_route_group, b_route_group, w_route_expert, b_route_expert, w_exp_gate, w_exp_up, w_exp_down, final_g):
    batch, seq, d = x.shape
    depth = w_in.shape[0]
    t = batch * seq
    x2 = x.reshape(t, d)
    mod = _ada_mod(c, w_ada, b_ada)
    rc, rs, mc, ms = _rope_tables(positions)
    for l in range(depth):
        mod3 = mod[l].reshape(batch * 6, 1, d)
        proj = _in_proj(x2, mod3, norm_mix_g[l].reshape(1, d), _prep_w_in(w_in[l]), seq)
        wa, wb = _prep_w_uq(w_uq[l])
        wk, wv = _prep_w_ukv(w_ukv[l])
        q, k, v = _mla_prep(proj, mc, ms, mla_q_norm_g[l].reshape(1, -1), mla_kv_norm_g[l].reshape(1, -1), wa, wb, wk, wv,
                            batch, seq)
        att = _attention(q, k, v, batch, seq)
        ret = _retention(proj, rc, rs, batch, seq)
        wrh, wrl, br = _prep_router(w_route_group[l], b_route_group[l], w_route_expert[l], b_route_expert[l])
        cw = jnp.pad(conv_w[l], ((0, SUBLANES - CONV_K), (0, 0)))
        x2, h, rout = _mix(proj, att, ret, x2, mod3, norm_ffn_g[l].reshape(1, d), cw,
                           w_o_mla[l].astype(BF16), w_o_conv[l].astype(BF16), w_o_ret[l].astype(BF16),
                           w_mix_out[l].astype(BF16), wrh, wrl, br, seq)
        moe_out = _moe(h, rout, w_exp_gate, w_exp_up, w_exp_down, l)
        x2 = _combine(x2, moe_out, mod3, final_g.reshape(1, d), seq, final=(l == depth - 1))
    return x2.reshape(batch, seq, d)
```

```python
import functools

import jax
import jax.numpy as jnp
from jax import lax
from jax.experimental import pallas as pl
from jax.experimental.pallas import tpu as pltpu

F32 = jnp.float32
BF16 = jnp.bfloat16

D_MODEL = 1024
CHUNK = 64
EPS = 1e-6
ROPE_THETA = 10000.0
LOG2E = 1.4426950408889634

MLA_HEADS = 8
MLA_Q_RANK = 384
MLA_KV_RANK = 256
MLA_NOPE = 64
MLA_ROPE = 32
MLA_V = 64

CONV_WIDTH = 512
CONV_K = 3

RET_HEADS = 4
RET_DK = 128
RET_DV = 256

N_GROUPS = 4
EXPERTS_PER_GROUP = 8
N_EXPERTS = N_GROUPS * EXPERTS_PER_GROUP
EXPERT_FF = 512

LANES = 128
SUBLANES = 8
VMEM_LIMIT = 52 * 1024 * 1024

GATE_W = 3 * D_MODEL
RQ_OFF = GATE_W
RK_OFF = RQ_OFF + RET_HEADS * RET_DK
RV_OFF = RK_OFF + RET_HEADS * RET_DK
RG_OFF = RV_OFF + RET_HEADS * RET_DV
CONV_OFF = RG_OFF + RET_HEADS * RET_DV
MLA_OFF = CONV_OFF + 3 * CONV_WIDTH
MLA_W = MLA_Q_RANK + MLA_KV_RANK + LANES
PROJ_W = MLA_OFF + MLA_W
PROJ_TN = PROJ_W // 6
assert PROJ_TN * 6 == PROJ_W and PROJ_TN % LANES == 0
assert MLA_OFF % MLA_W == 0 and CONV_OFF % (3 * CONV_WIDTH) == 0

QPAD = LANES
ROUTE_E_OFF = 32

ROW_TILE_PROJ = 1024
ROW_TILE = 256
ATT_TILE = 512
ATT_SUB = 256
LAZY_MAX_HEADROOM = 60.0
RET_BLOCK = 256
MOE_TILE = 256
ADA_TN = 768


def _params(sem, vmem=VMEM_LIMIT):
    return pltpu.CompilerParams(dimension_semantics=sem, vmem_limit_bytes=vmem)


def _const_spec(shape):
    nd = len(shape)
    return pl.BlockSpec(shape, lambda *_: (0,) * nd)


def _rms(x, g):
    return x * lax.rsqrt(jnp.mean(x * x, axis=-1, keepdims=True) + EPS) * g


def _sigmoid(x):
    return 1.0 / (1.0 + jnp.exp(-x))


def _ada_kernel(ct_ref, w_ref, b_ref, o_ref):
    ct = ct_ref[...]
    a = ct * _sigmoid(ct)
    w = w_ref[0]
    rows = [jnp.sum(w * a[:, b:b + 1], axis=0, keepdims=True) for b in range(ct.shape[1])]
    o_ref[0] = jnp.concatenate(rows, axis=0) + b_ref[0]


def _ada_mod(c, w_ada, b_ada):
    depth, d, n = w_ada.shape
    b = c.shape[0]
    return pl.pallas_call(
        _ada_kernel,
        grid=(depth, n // ADA_TN),
        in_specs=[
            _const_spec((d, b)),
            pl.BlockSpec((1, d, ADA_TN), lambda l, j: (l, 0, j)),
            pl.BlockSpec((1, 1, ADA_TN), lambda l, j: (l, 0, j)),
        ],
        out_specs=pl.BlockSpec((1, b, ADA_TN), lambda l, j: (l, 0, j)),
        out_shape=jax.ShapeDtypeStruct((depth, b, n), F32),
        compiler_params=_params(("arbitrary", "arbitrary")),
        name="ada_mod",
    )(c.T, w_ada, b_ada.reshape(depth, 1, n))


def _rope_kernel(pos_ref, f_ref, rc_ref, rs_ref, mc_ref, ms_ref):
    ang = pos_ref[...].astype(F32) * f_ref[...]
    cs = jnp.cos(ang)
    sn = jnp.sin(ang)
    lane = lax.broadcasted_iota(jnp.int32, ang.shape, 1)
    half = LANES // 2
    lo = lane < half
    rc_ref[...] = jnp.where(lo, cs, pltpu.roll(cs, half, 1))
    rs_ref[...] = jnp.where(lo, -sn, pltpu.roll(sn, half, 1))
    rope = (lane >= MLA_NOPE) & (lane < MLA_NOPE + MLA_ROPE)
    mc_ref[...] = jnp.where(lane < MLA_NOPE, 1.0, jnp.where(rope, cs, 0.0))
    ms_ref[...] = jnp.where(rope, sn, 0.0)


def _rope_tables(positions):
    t = positions.size
    tm = min(2048, t)
    inv_ret = ROPE_THETA ** (-jnp.arange(0, RET_DK, 2, dtype=F32) / RET_DK)
    inv_mla = ROPE_THETA ** (-jnp.arange(0, MLA_ROPE, 2, dtype=F32) / MLA_ROPE)
    freqs = jnp.concatenate([inv_ret, inv_mla, inv_mla, jnp.zeros((LANES - 96,), F32)]).reshape(1, LANES)
    spec = pl.BlockSpec((tm, LANES), lambda i: (i, 0))
    return pl.pallas_call(
        _rope_kernel,
        grid=(t // tm,),
        in_specs=[pl.BlockSpec((tm, 1), lambda i: (i, 0)), _const_spec((1, LANES))],
        out_specs=[spec] * 4,
        out_shape=[jax.ShapeDtypeStruct((t, LANES), F32)] * 4,
        compiler_params=_params(("arbitrary",)),
        name="rope_tables",
    )(positions.reshape(t, 1), freqs)


def _proj_kernel(x_ref, sh_ref, sc_ref, g_ref, w_ref, o_ref, h_scr):
    @pl.when(pl.program_id(1) == 0)
    def _():
        h = _rms(x_ref[...], g_ref[...]) * (1.0 + sc_ref[0]) + sh_ref[0]
        h_scr[...] = h.astype(BF16)

    o_ref[...] = jnp.dot(h_scr[...], w_ref[...], preferred_element_type=F32).astype(BF16)


def _in_proj(x2, mod3, g, w, seq):
    t, d = x2.shape
    tm = min(ROW_TILE_PROJ, seq)
    per_b = seq // tm
    return pl.pallas_call(
        _proj_kernel,
        grid=(t // tm, PROJ_W // PROJ_TN),
        in_specs=[
            pl.BlockSpec((tm, d), lambda i, j: (i, 0)),
            pl.BlockSpec((1, 1, d), lambda i, j: ((i // per_b) * 6 + 0, 0, 0)),
            pl.BlockSpec((1, 1, d), lambda i, j: ((i // per_b) * 6 + 1, 0, 0)),
            _const_spec((1, d)),
            pl.BlockSpec((d, PROJ_TN), lambda i, j: (0, j)),
        ],
        out_specs=pl.BlockSpec((tm, PROJ_TN), lambda i, j: (i, j)),
        out_shape=jax.ShapeDtypeStruct((t, PROJ_W), BF16),
        scratch_shapes=[pltpu.VMEM((tm, d), BF16)],
        compiler_params=_params(("arbitrary", "arbitrary")),
        name="in_proj",
    )(x2, mod3, mod3, g, w)


def _mla_prep_kernel(in_ref, mc_ref, ms_ref, gq_ref, gkv_ref, wa_ref, wb_ref, wk_ref, wv_ref, q_ref, k_ref, v_ref):
    blk = in_ref[...].astype(F32)
    qn = _rms(blk[:, :MLA_Q_RANK], gq_ref[...]).astype(BF16)
    kvn = _rms(blk[:, MLA_Q_RANK:MLA_Q_RANK + MLA_KV_RANK], gkv_ref[...]).astype(BF16)
    kb = blk[:, MLA_Q_RANK + MLA_KV_RANK:]
    mc = mc_ref[...]
    ms = ms_ref[...]
    qa = jnp.dot(qn, wa_ref[...], preferred_element_type=F32)
    qb = jnp.dot(qn, wb_ref[...], preferred_element_type=F32)
    kn = jnp.dot(kvn, wk_ref[...], preferred_element_type=F32)
    lane = lax.broadcasted_iota(jnp.int32, kb.shape, 1)
    rope = (lane >= MLA_NOPE) & (lane < MLA_NOPE + MLA_ROPE)
    kpe = jnp.where(rope, kb * mc + pltpu.roll(kb, LANES // 2, 1) * ms, 0.0)
    scale = (MLA_NOPE + MLA_ROPE) ** -0.5 * LOG2E
    for h in range(MLA_HEADS):
        sl = slice(h * QPAD, (h + 1) * QPAD)
        q_ref[:, sl] = ((qa[:, sl] * mc + qb[:, sl] * ms) * scale).astype(BF16)
        k_ref[:, sl] = (kn[:, sl] + kpe).astype(BF16)
    vt = lax.dot_general(wv_ref[...], kvn, (((1,), (1,)), ((), ())), preferred_element_type=F32)
    row = lax.broadcasted_iota(jnp.int32, vt.shape, 0)
    v_ref[0, 0] = jnp.where(row % QPAD >= MLA_V, 1.0, vt).astype(BF16)


def _mla_prep(proj, mc, ms, gq, gkv, wa, wb, wk, wv, batch, seq):
    t = proj.shape[0]
    tm = min(ATT_TILE, seq)
    per_b = seq // tm
    hq = MLA_HEADS * QPAD
    hv = MLA_HEADS * QPAD
    return pl.pallas_call(
        _mla_prep_kernel,
        grid=(t // tm,),
        in_specs=[
            pl.BlockSpec((tm, MLA_W), lambda i: (i, MLA_OFF // MLA_W)),
            pl.BlockSpec((tm, LANES), lambda i: (i, 0)),
            pl.BlockSpec((tm, LANES), lambda i: (i, 0)),
            _const_spec((1, MLA_Q_RANK)),
            _const_spec((1, MLA_KV_RANK)),
            _const_spec((MLA_Q_RANK, hq)),
            _const_spec((MLA_Q_RANK, hq)),
            _const_spec((MLA_KV_RANK, hq)),
            _const_spec((hv, MLA_KV_RANK)),
        ],
        out_specs=[
            pl.BlockSpec((tm, hq), lambda i: (i, 0)),
            pl.BlockSpec((tm, hq), lambda i: (i, 0)),
            pl.BlockSpec((1, 1, hv, tm), lambda i: (i // per_b, i % per_b, 0, 0)),
        ],
        out_shape=[
            jax.ShapeDtypeStruct((t, hq), BF16),
            jax.ShapeDtypeStruct((t, hq), BF16),
            jax.ShapeDtypeStruct((batch, per_b, hv, tm), BF16),
        ],
        compiler_params=_params(("arbitrary",)),
        name="mla_prep",
    )(proj, mc, ms, gq, gkv, wa, wb, wk, wv)


def _attn_kernel(q_ref, k_ref, vt_ref, o_ref, m_scr, acc_scr, *, tile):
    qi = pl.program_id(2)
    nn_dims = (((1,), (1,)), ((), ()))

    def head_q(a):
        return q_ref[0, :, a * QPAD:(a + 1) * QPAD]

    def head_k(start, size, a):
        return k_ref[0, pl.ds(start, size), a * QPAD:(a + 1) * QPAD]

    def exact_step(j, src, dst, first, masked):
        start = pl.multiple_of(j * tile, tile)
        for a in range(2):
            st = lax.dot_general(head_k(start, tile, a), head_q(a), nn_dims, preferred_element_type=F32)
            if masked:
                key = lax.broadcasted_iota(jnp.int32, (tile, tile), 0)
                qry = lax.broadcasted_iota(jnp.int32, (tile, tile), 1)
                st = jnp.where((key // CHUNK) <= (qry // CHUNK), st, -jnp.inf)
            m_new = jnp.max(st, axis=0, keepdims=True)
            if not first:
                m_old = m_scr[src, a]
                m_new = jnp.maximum(m_old, m_new)
            p = jnp.exp2(st - m_new).astype(BF16)
            vt = vt_ref[0, j, a * QPAD:(a + 1) * QPAD, :]
            pv = jnp.dot(vt, p, preferred_element_type=F32)
            if not first:
                pv = pv + jnp.exp2(m_old - m_new) * acc_scr[src, a]
            acc_scr[dst, a] = pv
            m_scr[dst, a] = m_new

    exact_step(qi, 0, 0, first=True, masked=True)

    def body(j, cur):
        start = pl.multiple_of(j * tile, tile)
        nxt = 1 - cur
        n_sub = tile // ATT_SUB
        scores = {}
        for a in range(2):
            for s in range(n_sub):
                k = head_k(start + s * ATT_SUB, ATT_SUB, a)
                scores[a, s] = lax.dot_general(k, head_q(a), nn_dims, preferred_element_type=F32)
        excess = jnp.zeros((1, tile), F32)
        for a in range(2):
            m_used = m_scr[cur, a]
            m_new = m_used
            pv = jnp.zeros((QPAD, tile), F32)
            for s in range(n_sub):
                st = scores[a, s]
                m_new = jnp.maximum(m_new, jnp.max(st, axis=0, keepdims=True))
                p = jnp.exp2(st - m_used).astype(BF16)
                vt = vt_ref[0, j, a * QPAD:(a + 1) * QPAD, s * ATT_SUB:(s + 1) * ATT_SUB]
                pv = pv + jnp.dot(vt, p, preferred_element_type=F32)
            acc_scr[nxt, a] = (acc_scr[cur, a] + pv) * jnp.exp2(m_used - m_new)
            m_scr[nxt, a] = m_new
            excess = jnp.maximum(excess, m_new - m_used)

        @pl.when(jnp.max(excess) > LAZY_MAX_HEADROOM)
        def _():
            exact_step(j, cur, nxt, first=False, masked=False)

        return nxt

    cur = lax.fori_loop(0, qi, body, 0)

    lane = lax.broadcasted_iota(jnp.int32, (tile, LANES), 1)
    r0 = acc_scr[cur, 0].T
    r1 = acc_scr[cur, 1].T
    o0 = r0 / r0[:, MLA_V:MLA_V + 1]
    o1 = pltpu.roll(r1, MLA_V, 1) / r1[:, MLA_V:MLA_V + 1]
    o_ref[0] = jnp.where(lane < MLA_V, o0, o1).astype(BF16)


def _attention(q, k, vt, batch, seq):
    tile = min(ATT_TILE, seq)
    q3 = q.reshape(batch, seq, MLA_HEADS * QPAD)
    k3 = k.reshape(batch, seq, MLA_HEADS * QPAD)
    out = pl.pallas_call(
        functools.partial(_attn_kernel, tile=tile),
        grid=(batch, MLA_HEADS // 2, seq // tile),
        in_specs=[
            pl.BlockSpec((1, tile, 2 * QPAD), lambda b, h, i: (b, i, h)),
            pl.BlockSpec((1, seq, 2 * QPAD), lambda b, h, i: (b, 0, h)),
            pl.BlockSpec((1, seq // tile, 2 * QPAD, tile), lambda b, h, i: (b, 0, h, 0)),
        ],
        out_specs=pl.BlockSpec((1, tile, 2 * MLA_V), lambda b, h, i: (b, i, h)),
        out_shape=jax.ShapeDtypeStruct((batch, seq, MLA_HEADS * MLA_V), BF16),
        scratch_shapes=[
            pltpu.VMEM((2, 2, 1, tile), F32),
            pltpu.VMEM((2, 2, QPAD, tile), F32),
        ],
        compiler_params=_params(("arbitrary", "arbitrary", "arbitrary")),
        name="mla_attention",
    )(q3, k3, vt)
    return out.reshape(batch * seq, MLA_HEADS * MLA_V)


def _ret_kernel(lg_ref, q_ref, k_ref, v_ref, g_ref, rc_ref, rs_ref, o_ref, state_scr, decay_scr, *, blk):
    n = pl.program_id(1)

    @pl.when(n == 0)
    def _():
        state_scr[...] = jnp.zeros(state_scr.shape, F32)
        row = lax.broadcasted_iota(jnp.int32, (blk, blk), 0)
        col = lax.broadcasted_iota(jnp.int32, (blk, blk), 1)
        dist = jnp.abs(row - col).astype(F32)
        visible = (col // CHUNK) <= (row // CHUNK)
        for h in range(RET_HEADS):
            decay_scr[h] = jnp.where(visible, jnp.exp(lg_ref[h] * dist), 0.0)

    rc = rc_ref[...]
    rs = rs_ref[...]
    half = RET_DK // 2
    pos = lax.broadcasted_iota(jnp.int32, (blk, 1), 0).astype(F32)
    for h in range(RET_HEADS):
        lg = lg_ref[h]
        ksl = slice(h * RET_DK, (h + 1) * RET_DK)
        vsl = slice(h * RET_DV, (h + 1) * RET_DV)
        qf = q_ref[:, ksl].astype(F32)
        kf = k_ref[:, ksl].astype(F32)
        q = qf * rc + pltpu.roll(qf, half, 1) * rs
        k = (kf * rc + pltpu.roll(kf, half, 1) * rs) * (RET_DK ** -0.5)
        v = v_ref[:, vsl]
        q_dec = jnp.exp(lg * (pos + 1.0))
        k_dec = jnp.exp(lg * (blk - 1.0 - pos))

        s = lax.dot_general(q.astype(BF16), k.astype(BF16), (((1,), (1,)), ((), ())), preferred_element_type=F32)
        s = s * decay_scr[h]
        state = state_scr[h]
        o = jnp.dot(s.astype(BF16), v, preferred_element_type=F32)
        o = o + jnp.dot((q * q_dec).astype(BF16), state.astype(BF16), preferred_element_type=F32)
        kd = (k * k_dec).astype(BF16)
        upd = lax.dot_general(kd, v, (((0,), (0,)), ((), ())), preferred_element_type=F32)
        state_scr[h] = state * jnp.exp(lg * blk) + upd

        o = o * lax.rsqrt(jnp.mean(o * o, axis=-1, keepdims=True) + EPS)
        g = g_ref[:, vsl].astype(F32)
        o_ref[:, vsl] = (g * _sigmoid(g) * o).astype(BF16)


def _retention(proj, rc, rs, batch, seq):
    t = proj.shape[0]
    blk = min(RET_BLOCK, seq)
    per_b = seq // blk
    log_gamma = jnp.log(1.0 - 2.0 ** (-5.0 - jnp.arange(RET_HEADS, dtype=F32)))
    qk_w = RET_HEADS * RET_DK
    v_w = RET_HEADS * RET_DV
    row = lambda b, n: b * per_b + n
    return pl.pallas_call(
        functools.partial(_ret_kernel, blk=blk),
        grid=(batch, per_b),
        in_specs=[
            pl.BlockSpec(memory_space=pltpu.SMEM),
            pl.BlockSpec((blk, qk_w), lambda b, n: (row(b, n), RQ_OFF // qk_w)),
            pl.BlockSpec((blk, qk_w), lambda b, n: (row(b, n), RK_OFF // qk_w)),
            pl.BlockSpec((blk, v_w), lambda b, n: (row(b, n), RV_OFF // v_w)),
            pl.BlockSpec((blk, v_w), lambda b, n: (row(b, n), RG_OFF // v_w)),
            pl.BlockSpec((blk, LANES), lambda b, n: (row(b, n), 0)),
            pl.BlockSpec((blk, LANES), lambda b, n: (row(b, n), 0)),
        ],
        out_specs=pl.BlockSpec((blk, v_w), lambda b, n: (row(b, n), 0)),
        out_shape=jax.ShapeDtypeStruct((t, v_w), BF16),
        scratch_shapes=[pltpu.VMEM((RET_HEADS, RET_DK, RET_DV), F32), pltpu.VMEM((RET_HEADS, blk, blk), F32)],
        compiler_params=_params(("arbitrary", "arbitrary")),
        name="retention",
    )(log_gamma, proj, proj, proj, proj, rc, rs)


def _mix_kernel(gate_ref, conv_ref, att_ref, ret_ref, x_ref, gt_ref, sh_ref, sc_ref, gf_ref, cw_ref,
                wom_ref, woc_ref, wor_ref, wmo_ref, wrh_ref, wrl_ref, br_ref,
                xo_ref, h_ref, rout_ref, cnt_ref, u_scr, *, tm, per_b):
    i = pl.program_id(0)
    d = D_MODEL
    cw = CONV_WIDTH

    cb = conv_ref[:, 0:cw].astype(F32)
    u = conv_ref[:, cw:2 * cw].astype(F32) * conv_ref[:, 2 * cw:3 * cw].astype(F32)

    @pl.when(i % per_b == 0)
    def _():
        u_scr[0:SUBLANES, :] = jnp.zeros((SUBLANES, cw), F32)

    @pl.when(i % per_b != 0)
    def _():
        u_scr[0:SUBLANES, :] = u_scr[tm:tm + SUBLANES, :]

    u_scr[SUBLANES:SUBLANES + tm, :] = u
    y = cw_ref[2:3, :] * u
    y = y + cw_ref[1:2, :] * u_scr[SUBLANES - 1:SUBLANES - 1 + tm, :]
    y = y + cw_ref[0:1, :] * u_scr[SUBLANES - 2:SUBLANES - 2 + tm, :]
    z = (cb * y).astype(BF16)

    y_mla = jnp.dot(att_ref[...], wom_ref[...], preferred_element_type=F32)
    y_conv = jnp.dot(z, woc_ref[...], preferred_element_type=F32)
    y_ret = jnp.dot(ret_ref[...], wor_ref[...], preferred_element_type=F32)
    merged = _sigmoid(gate_ref[:, 0:d].astype(F32)) * y_mla
    merged = merged + _sigmoid(gate_ref[:, d:2 * d].astype(F32)) * y_conv
    merged = merged + _sigmoid(gate_ref[:, 2 * d:3 * d].astype(F32)) * y_ret
    x = x_ref[...] + gt_ref[0] * jnp.dot(merged.astype(BF16), wmo_ref[...], preferred_element_type=F32)
    xo_ref[...] = x

    h = _rms(x, gf_ref[...]) * (1.0 + sc_ref[0]) + sh_ref[0]
    h_ref[...] = h

    h_hi = h.astype(BF16)
    h_lo = (h - h_hi.astype(F32)).astype(BF16)
    lg = jnp.dot(h_hi, wrh_ref[...], preferred_element_type=F32)
    lg = lg + jnp.dot(h_lo, wrh_ref[...], preferred_element_type=F32)
    lg = lg + jnp.dot(h_hi, wrl_ref[...], preferred_element_type=F32)
    lg = lg + br_ref[...]

    lane = lax.broadcasted_iota(jnp.int32, lg.shape, 1)
    neg = -jnp.inf
    gl = jnp.where(lane < N_GROUPS, lg, neg)
    gmax = jnp.max(gl, axis=-1, keepdims=True)
    g_val = 1.0 / jnp.sum(jnp.exp(gl - gmax), axis=-1, keepdims=True)
    g_idx = jnp.min(jnp.where(gl == gmax, lane, LANES), axis=-1, keepdims=True)
    in_group = (lane >= ROUTE_E_OFF) & (lane < ROUTE_E_OFF + N_EXPERTS)
    in_group = in_group & (((lane - ROUTE_E_OFF) // EXPERTS_PER_GROUP) == g_idx)
    el = jnp.where(in_group, lg, neg)
    m1 = jnp.max(el, axis=-1, keepdims=True)
    i1 = jnp.min(jnp.where(el == m1, lane, LANES), axis=-1, keepdims=True)
    el2 = jnp.where(lane == i1, neg, el)
    m2 = jnp.max(el2, axis=-1, keepdims=True)
    i2 = jnp.min(jnp.where(el2 == m2, lane, LANES), axis=-1, keepdims=True)
    r = jnp.exp(m2 - m1)
    w1 = g_val / (1.0 + r)
    w2 = g_val * r / (1.0 + r)
    e1 = (i1 - ROUTE_E_OFF).astype(F32)
    e2 = (i2 - ROUTE_E_OFF).astype(F32)
    rout = jnp.where(lane == 0, e1, jnp.where(lane == 1, e2, jnp.where(lane == 2, w1, jnp.where(lane == 3, w2, 0.0))))
    rout_ref[...] = rout.T[0:SUBLANES, :]

    hits = jnp.where(lane == i1, 1.0, 0.0) + jnp.where(lane == i2, 1.0, 0.0)

    @pl.when(i == 0)
    def _():
        cnt_ref[...] = jnp.zeros(cnt_ref.shape, F32)

    cnt_ref[0:1, :] = cnt_ref[0:1, :] + jnp.sum(hits, axis=0, keepdims=True)


def _mix(proj, att, ret, x2, mod3, gf, cw, wom, woc, wor, wmo, wrh, wrl, br, seq):
    t, d = x2.shape
    tm = min(ROW_TILE, seq)
    per_b = seq // tm
    row = lambda i: (i, 0)
    modspec = lambda k: pl.BlockSpec((1, 1, d), lambda i: ((i // per_b) * 6 + k, 0, 0))
    return pl.pallas_call(
        functools.partial(_mix_kernel, tm=tm, per_b=per_b),
        grid=(t // tm,),
        in_specs=[
            pl.BlockSpec((tm, GATE_W), row),
            pl.BlockSpec((tm, 3 * CONV_WIDTH), lambda i: (i, CONV_OFF // (3 * CONV_WIDTH))),
            pl.BlockSpec((tm, MLA_HEADS * MLA_V), row),
            pl.BlockSpec((tm, RET_HEADS * RET_DV), row),
            pl.BlockSpec((tm, d), row),
            modspec(2), modspec(3), modspec(4),
            _const_spec((1, d)),
            _const_spec((SUBLANES, CONV_WIDTH)),
            _const_spec(wom.shape), _const_spec(woc.shape), _const_spec(wor.shape), _const_spec(wmo.shape),
            _const_spec(wrh.shape), _const_spec(wrl.shape), _const_spec((1, LANES)),
        ],
        out_specs=[pl.BlockSpec((tm, d), row), pl.BlockSpec((tm, d), row),
                   pl.BlockSpec((SUBLANES, tm), lambda i: (0, i)), _const_spec((SUBLANES, LANES))],
        out_shape=[
            jax.ShapeDtypeStruct((t, d), F32),
            jax.ShapeDtypeStruct((t, d), F32),
            jax.ShapeDtypeStruct((SUBLANES, t), F32),
            jax.ShapeDtypeStruct((SUBLANES, LANES), F32),
        ],
        scratch_shapes=[pltpu.VMEM((tm + SUBLANES, CONV_WIDTH), F32)],
        compiler_params=_params(("arbitrary",)),
        name="mix_router",
    )(proj, proj, att, ret, x2, mod3, mod3, mod3, gf, cw, wom, woc, wor, wmo, wrh, wrl, br)


def _moe_kernel(te_ref, nv_ref, meta_ref, meta_next_ref, wt_ref, wg_ref, wu_ref, wd_ref, h_hbm, out_hbm,
                xbuf, ybuf, wgb, wub, wdb, gsem, ssem, *, tm):
    i = pl.program_id(0)
    nt = pl.num_programs(0)
    slot = i % 2
    nv = nv_ref[i]
    prev = jnp.maximum(i - 1, 0)
    prev2 = jnp.maximum(i - 2, 0)

    def gather_copy(tok, s, r):
        return pltpu.make_async_copy(h_hbm.at[pl.ds(tok, 1)], xbuf.at[s, pl.ds(r, 1)], gsem.at[s])

    def scatter_copy(dst, s, r):
        return pltpu.make_async_copy(ybuf.at[s, pl.ds(r, 1)], out_hbm.at[pl.ds(dst, 1)], ssem.at[s])

    def start_gather(meta, s):
        for r in range(tm):
            gather_copy(meta[0, 0, r], s, r).start(priority=r % 2)

    def wait_gather(s):
        for r in range(tm):
            gather_copy(0, s, r).wait()

    def start_scatter(s):
        for r in range(tm):
            scatter_copy(meta_ref[0, 0, tm + r], s, r).start(priority=r % 2)

    def wait_scatter(s):
        for r in range(tm):
            scatter_copy(0, s, r).wait()

    def for_slot(fn):
        @pl.when(slot == 0)
        def _():
            fn(0)

        @pl.when(slot == 1)
        def _():
            fn(1)

    @pl.when(i == 0)
    def _():
        ybuf[...] = jnp.zeros(ybuf.shape, F32)
        n_pad = 2 * tm
        for s in range(2):
            pad_rows = out_hbm.at[pl.ds(out_hbm.shape[0] - n_pad + s * tm, tm)]
            init = pltpu.make_async_copy(ybuf.at[s], pad_rows, ssem.at[s])
            init.start()
            init.wait()
        start_gather(meta_ref, 0)

    @pl.when((i >= 2) & (nv_ref[prev2] > 0))
    def _():
        for_slot(wait_scatter)

    @pl.when((i == 0) | (te_ref[i] != te_ref[prev]))
    def _():
        wgb[...] = wg_ref[0].astype(BF16)
        wub[...] = wu_ref[0].astype(BF16)
        wdb[...] = wd_ref[0].astype(BF16)

    def active(s):
        wait_gather(s)
        x = xbuf[s].astype(BF16)
        start_gather(meta_next_ref, 1 - s)
        g = jnp.dot(x, wgb[...], preferred_element_type=F32)
        u = jnp.dot(x, wub[...], preferred_element_type=F32)
        hid = (g * _sigmoid(g) * u).astype(BF16)
        ybuf[s] = jnp.dot(hid, wdb[...], preferred_element_type=F32) * wt_ref[0]
        start_scatter(s)

    @pl.when(nv > 0)
    def _():
        for_slot(active)

    @pl.when((i > 0) & (nv == 0) & (nv_ref[prev] > 0))
    def _():
        for_slot(wait_gather)

    @pl.when((i == nt - 1) & (nv_ref[prev] > 0))
    def _():
        for_slot(lambda s: wait_scatter(1 - s))


def _route_tables(rout, cnt, t, tm, nt):
    e = rout[0:2].astype(jnp.int32).reshape(-1)
    w = rout[2:4].reshape(-1)
    n_slot = 2 * t
    perm = jnp.argsort(e).astype(jnp.int32)
    counts = cnt[0, ROUTE_E_OFF:ROUTE_E_OFF + N_EXPERTS].astype(jnp.int32)
    nt_e = (counts + tm - 1) // tm
    tile_end = jnp.cumsum(nt_e)
    tile_start = tile_end - nt_e
    off = jnp.cumsum(counts) - counts
    total = tile_end[-1]
    i = jnp.arange(nt, dtype=jnp.int32)
    e_raw = jnp.minimum(jnp.sum((tile_end[None, :] <= i[:, None]).astype(jnp.int32), axis=1), N_EXPERTS - 1)
    e_i = jnp.where(i < total, e_raw, e_raw[jnp.maximum(total - 1, 0)])
    j = i - tile_start[e_i]
    nvalid = jnp.where(i < total, jnp.clip(counts[e_i] - j * tm, 0, tm), 0).astype(jnp.int32)
    r = jnp.arange(tm, dtype=jnp.int32)
    pos = jnp.clip((off[e_i] + j * tm)[:, None] + r[None, :], 0, n_slot - 1)
    slot = perm[pos]
    valid = r[None, :] < nvalid[:, None]
    tok = jnp.where(slot >= t, slot - t, slot)
    dst = jnp.where(valid, slot, n_slot + (i % 2)[:, None] * tm + r[None, :])
    wt = jnp.where(valid, w[slot], 0.0)
    meta = jnp.concatenate([tok, dst], axis=1).astype(jnp.int32).reshape(nt, 1, 2 * tm)
    return e_i, nvalid, meta, wt.reshape(nt, tm, 1)


def _moe(h, rout, cnt, wg, wu, wd, layer):
    t, d = h.shape
    tm = min(MOE_TILE, t)
    nt = (2 * t) // tm + N_EXPERTS
    e_i, nvalid, meta, wt = _route_tables(rout, cnt, t, tm, nt)
    f = wg.shape[-1]
    grid_spec = pltpu.PrefetchScalarGridSpec(
        num_scalar_prefetch=2,
        grid=(nt,),
        in_specs=[
            pl.BlockSpec((1, 1, 2 * tm), lambda i, te, nv: (i, 0, 0), memory_space=pltpu.SMEM),
            pl.BlockSpec((1, 1, 2 * tm), lambda i, te, nv: (jnp.minimum(i + 1, nt - 1), 0, 0), memory_space=pltpu.SMEM),
            pl.BlockSpec((1, tm, 1), lambda i, te, nv: (i, 0, 0)),
            pl.BlockSpec((1, 1, d, f), lambda i, te, nv: (layer, te[i], 0, 0)),
            pl.BlockSpec((1, 1, d, f), lambda i, te, nv: (layer, te[i], 0, 0)),
            pl.BlockSpec((1, 1, f, d), lambda i, te, nv: (layer, te[i], 0, 0)),
            pl.BlockSpec(memory_space=pl.ANY),
        ],
        out_specs=pl.BlockSpec(memory_space=pl.ANY),
        scratch_shapes=[
            pltpu.VMEM((2, tm, d), F32),
            pltpu.VMEM((2, tm, d), F32),
            pltpu.VMEM((d, f), BF16),
            pltpu.VMEM((d, f), BF16),
            pltpu.VMEM((f, d), BF16),
            pltpu.SemaphoreType.DMA((2,)),
            pltpu.SemaphoreType.DMA((2,)),
        ],
    )

    def kern(te_ref, nv_ref, meta_ref, meta_next_ref, wt_ref, wg_ref, wu_ref, wd_ref, h_hbm, out_hbm, *scratch):
        _moe_kernel(te_ref, nv_ref, meta_ref, meta_next_ref, wt_ref, wg_ref.at[0], wu_ref.at[0], wd_ref.at[0],
                    h_hbm, out_hbm, *scratch, tm=tm)

    return pl.pallas_call(
        kern,
        grid_spec=grid_spec,
        out_shape=jax.ShapeDtypeStruct((2 * t + 2 * tm, d), F32),
        compiler_params=_params(("arbitrary",)),
        name="moe_experts",
    )(e_i, nvalid, meta, meta, wt, wg, wu, wd, h)


def _combine_kernel(x_ref, a_ref, b_ref, gt_ref, g_ref, o_ref, *, final):
    x = x_ref[...] + gt_ref[0] * (a_ref[...] + b_ref[...])
    if final:
        x = _rms(x, g_ref[...])
    o_ref[...] = x


def _combine(x2, moe_out, mod3, final_g, seq, final):
    t, d = x2.shape
    tm = min(512, seq)
    per_b = seq // tm
    return pl.pallas_call(
        functools.partial(_combine_kernel, final=final),
        grid=(t // tm,),
        in_specs=[
            pl.BlockSpec((tm, d), lambda i: (i, 0)),
            pl.BlockSpec((tm, d), lambda i: (i, 0)),
            pl.BlockSpec((tm, d), lambda i: (t // tm + i, 0)),
            pl.BlockSpec((1, 1, d), lambda i: ((i // per_b) * 6 + 5, 0, 0)),
            _const_spec((1, d)),
        ],
        out_specs=pl.BlockSpec((tm, d), lambda i: (i, 0)),
        out_shape=jax.ShapeDtypeStruct((t, d), F32),
        compiler_params=_params(("arbitrary",)),
        name="combine",
    )(x2, moe_out, moe_out, mod3, final_g)


def _prep_w_in(w):
    d = w.shape[0]
    sizes = (MLA_Q_RANK, MLA_KV_RANK, MLA_ROPE, CONV_WIDTH, CONV_WIDTH, CONV_WIDTH,
             RET_HEADS * RET_DK, RET_HEADS * RET_DK, RET_HEADS * RET_DV, RET_HEADS * RET_DV, 3 * D_MODEL)
    parts = []
    start = 0
    for size in sizes:
        parts.append(w[:, start:start + size])
        start += size
    q_lat, kv_lat, kr, cb, cc, cx, rq, rk, rv, rg, gl = parts
    half = MLA_ROPE // 2
    zeros = lambda n: jnp.zeros((d, n), w.dtype)
    kblock = jnp.concatenate(
        [-kr[:, half:], kr[:, :half], zeros(MLA_NOPE - MLA_ROPE), kr, zeros(LANES - MLA_NOPE - MLA_ROPE)], axis=1)
    out = jnp.concatenate([gl, rq, rk, rv, rg, cb, cc, cx, q_lat, kv_lat, kblock], axis=1)
    return out.astype(BF16)


def _prep_w_uq(w):
    r = w.shape[0]
    dq = MLA_NOPE + MLA_ROPE
    half = MLA_ROPE // 2
    w3 = w.reshape(r, MLA_HEADS, dq)
    pad = QPAD - dq
    wa = jnp.pad(w3, ((0, 0), (0, 0), (0, pad))).reshape(r, MLA_HEADS * QPAD)
    wb = jnp.concatenate(
        [jnp.zeros((r, MLA_HEADS, MLA_NOPE), w.dtype), -w3[:, :, MLA_NOPE + half:], w3[:, :, MLA_NOPE:MLA_NOPE + half],
         jnp.zeros((r, MLA_HEADS, pad), w.dtype)], axis=2).reshape(r, MLA_HEADS * QPAD)
    return wa.astype(BF16), wb.astype(BF16)


def _prep_w_ukv(w):
    r = w.shape[0]
    w3 = w.reshape(r, MLA_HEADS, MLA_NOPE + MLA_V)
    wk = jnp.pad(w3[:, :, :MLA_NOPE], ((0, 0), (0, 0), (0, QPAD - MLA_NOPE))).reshape(r, MLA_HEADS * QPAD)
    wv = jnp.pad(w3[:, :, MLA_NOPE:], ((0, 0), (0, 0), (0, QPAD - MLA_V))).reshape(r, MLA_HEADS * QPAD)
    return wk.astype(BF16), wv.T.astype(BF16)


def _prep_router(w_rg, b_rg, w_re, b_re):
    d = w_rg.shape[0]
    w = jnp.zeros((d, LANES), F32)
    w = w.at[:, :N_GROUPS].set(w_rg).at[:, ROUTE_E_OFF:ROUTE_E_OFF + N_EXPERTS].set(w_re)
    b = jnp.zeros((1, LANES), F32)
    b = b.at[0, :N_GROUPS].set(b_rg).at[0, ROUTE_E_OFF:ROUTE_E_OFF + N_EXPERTS].set(b_re)
    w_hi = w.astype(BF16)
    w_lo = (w - w_hi.astype(F32)).astype(BF16)
    return w_hi, w_lo, b


def kernel(x, c, positions, w_ada, b_ada, norm_mix_g, norm_ffn_g, w_in, mla_q_norm_g, mla_kv_norm_g, w_uq, w_ukv, w_o_mla, conv_w, w_o_conv, w_o_ret, w_mix_out, w_route_group, b_route_group, w_route_expert, b_route_expert, w_exp_gate, w_exp_up, w_exp_down, final_g):
    batch, seq, d = x.shape
    depth = w_in.shape[0]
    t = batch * seq
    x2 = x.reshape(t, d)
    mod = _ada_mod(c, w_ada, b_ada)
    rc, rs, mc, ms = _rope_tables(positions)
    for l in range(depth):
        mod3 = mod[l].reshape(batch * 6, 1, d)
        proj = _in_proj(x2, mod3, norm_mix_g[l].reshape(1, d), _prep_w_in(w_in[l]), seq)
        wa, wb = _prep_w_uq(w_uq[l])
        wk, wv = _prep_w_ukv(w_ukv[l])
        q, k, v = _mla_prep(proj, mc, ms, mla_q_norm_g[l].reshape(1, -1), mla_kv_norm_g[l].reshape(1, -1), wa, wb, wk, wv,
                            batch, seq)
        att = _attention(q, k, v, batch, seq)
        ret = _retention(proj, rc, rs, batch, seq)
        wrh, wrl, br = _prep_router(w_route_group[l], b_route_group[l], w_route_expert[l], b_route_expert[l])
        cw = jnp.pad(conv_w[l], ((0, SUBLANES - CONV_K), (0, 0)))
        x2, h, rout, cnt = _mix(proj, att, ret, x2, mod3, norm_ffn_g[l].reshape(1, d), cw,
                           w_o_mla[l].astype(BF16), w_o_conv[l].astype(BF16), w_o_ret[l].astype(BF16),
                           w_mix_out[l].astype(BF16), wrh, wrl, br, seq)
        moe_out = _moe(h, rout, cnt, w_exp_gate, w_exp_up, w_exp_down, l)
        x2 = _combine(x2, moe_out, mod3, final_g.reshape(1, d), seq, final=(l == depth - 1))
    return x2.reshape(batch, seq, d)
```

```python
import functools

import jax
import jax.numpy as jnp
from jax import lax
from jax.experimental import pallas as pl
from jax.experimental.pallas import tpu as pltpu

F32 = jnp.float32
BF16 = jnp.bfloat16

D_MODEL = 1024
CHUNK = 64
EPS = 1e-6
ROPE_THETA = 10000.0
LOG2E = 1.4426950408889634

MLA_HEADS = 8
MLA_Q_RANK = 384
MLA_KV_RANK = 256
MLA_NOPE = 64
MLA_ROPE = 32
MLA_V = 64

CONV_WIDTH = 512
CONV_K = 3

RET_HEADS = 4
RET_DK = 128
RET_DV = 256

N_GROUPS = 4
EXPERTS_PER_GROUP = 8
N_EXPERTS = N_GROUPS * EXPERTS_PER_GROUP
EXPERT_FF = 512

LANES = 128
SUBLANES = 8
VMEM_LIMIT = 52 * 1024 * 1024

GATE_W = 3 * D_MODEL
RQ_OFF = GATE_W
RK_OFF = RQ_OFF + RET_HEADS * RET_DK
RV_OFF = RK_OFF + RET_HEADS * RET_DK
RG_OFF = RV_OFF + RET_HEADS * RET_DV
CONV_OFF = RG_OFF + RET_HEADS * RET_DV
MLA_OFF = CONV_OFF + 3 * CONV_WIDTH
MLA_W = MLA_Q_RANK + MLA_KV_RANK + LANES
PROJ_W = MLA_OFF + MLA_W
PROJ_TN = PROJ_W // 6
assert PROJ_TN * 6 == PROJ_W and PROJ_TN % LANES == 0
assert MLA_OFF % MLA_W == 0 and CONV_OFF % (3 * CONV_WIDTH) == 0

QPAD = LANES
ROUTE_E_OFF = 32

ROW_TILE_PROJ = 1024
ROW_TILE = 256
ATT_TILE = 512
ATT_SUB = 256
LAZY_MAX_HEADROOM = 60.0
RET_BLOCK = 256
MOE_TILE = 256
ADA_TN = 768


def _params(sem, vmem=VMEM_LIMIT):
    return pltpu.CompilerParams(dimension_semantics=sem, vmem_limit_bytes=vmem)


def _const_spec(shape):
    nd = len(shape)
    return pl.BlockSpec(shape, lambda *_: (0,) * nd)


ROW_SPLIT = D_MODEL // LANES
assert ROW_SPLIT == SUBLANES


def _store_token_tiles(ref, base, rows, val):
    for s in range(ROW_SPLIT):
        ref[pl.ds(base + s, rows, stride=ROW_SPLIT), :] = val[:, s * LANES:(s + 1) * LANES]


def _load_token_tiles(ref, base, rows):
    return jnp.concatenate([ref[pl.ds(base + s, rows, stride=ROW_SPLIT), :] for s in range(ROW_SPLIT)], axis=1)


def _rms(x, g):
    return x * lax.rsqrt(jnp.mean(x * x, axis=-1, keepdims=True) + EPS) * g


def _sigmoid(x):
    return 1.0 / (1.0 + jnp.exp(-x))


def _ada_kernel(ct_ref, w_ref, b_ref, o_ref):
    ct = ct_ref[...]
    a = ct * _sigmoid(ct)
    w = w_ref[0]
    rows = [jnp.sum(w * a[:, b:b + 1], axis=0, keepdims=True) for b in range(ct.shape[1])]
    o_ref[0] = jnp.concatenate(rows, axis=0) + b_ref[0]


def _ada_mod(c, w_ada, b_ada):
    depth, d, n = w_ada.shape
    b = c.shape[0]
    return pl.pallas_call(
        _ada_kernel,
        grid=(depth, n // ADA_TN),
        in_specs=[
            _const_spec((d, b)),
            pl.BlockSpec((1, d, ADA_TN), lambda l, j: (l, 0, j)),
            pl.BlockSpec((1, 1, ADA_TN), lambda l, j: (l, 0, j)),
        ],
        out_specs=pl.BlockSpec((1, b, ADA_TN), lambda l, j: (l, 0, j)),
        out_shape=jax.ShapeDtypeStruct((depth, b, n), F32),
        compiler_params=_params(("arbitrary", "arbitrary")),
        name="ada_mod",
    )(c.T, w_ada, b_ada.reshape(depth, 1, n))


def _rope_kernel(pos_ref, f_ref, rc_ref, rs_ref, mc_ref, ms_ref):
    ang = pos_ref[...].astype(F32) * f_ref[...]
    cs = jnp.cos(ang)
    sn = jnp.sin(ang)
    lane = lax.broadcasted_iota(jnp.int32, ang.shape, 1)
    half = LANES // 2
    lo = lane < half
    rc_ref[...] = jnp.where(lo, cs, pltpu.roll(cs, half, 1))
    rs_ref[...] = jnp.where(lo, -sn, pltpu.roll(sn, half, 1))
    rope = (lane >= MLA_NOPE) & (lane < MLA_NOPE + MLA_ROPE)
    mc_ref[...] = jnp.where(lane < MLA_NOPE, 1.0, jnp.where(rope, cs, 0.0))
    ms_ref[...] = jnp.where(rope, sn, 0.0)


def _rope_tables(positions):
    t = positions.size
    tm = min(2048, t)
    inv_ret = ROPE_THETA ** (-jnp.arange(0, RET_DK, 2, dtype=F32) / RET_DK)
    inv_mla = ROPE_THETA ** (-jnp.arange(0, MLA_ROPE, 2, dtype=F32) / MLA_ROPE)
    freqs = jnp.concatenate([inv_ret, inv_mla, inv_mla, jnp.zeros((LANES - 96,), F32)]).reshape(1, LANES)
    spec = pl.BlockSpec((tm, LANES), lambda i: (i, 0))
    return pl.pallas_call(
        _rope_kernel,
        grid=(t // tm,),
        in_specs=[pl.BlockSpec((tm, 1), lambda i: (i, 0)), _const_spec((1, LANES))],
        out_specs=[spec] * 4,
        out_shape=[jax.ShapeDtypeStruct((t, LANES), F32)] * 4,
        compiler_params=_params(("arbitrary",)),
        name="rope_tables",
    )(positions.reshape(t, 1), freqs)


def _proj_kernel(x_ref, sh_ref, sc_ref, g_ref, w_ref, o_ref, h_scr):
    @pl.when(pl.program_id(1) == 0)
    def _():
        h = _rms(x_ref[...], g_ref[...]) * (1.0 + sc_ref[0]) + sh_ref[0]
        h_scr[...] = h.astype(BF16)

    o_ref[...] = jnp.dot(h_scr[...], w_ref[...], preferred_element_type=F32).astype(BF16)


def _in_proj(x2, mod3, g, w, seq):
    t, d = x2.shape
    tm = min(ROW_TILE_PROJ, seq)
    per_b = seq // tm
    return pl.pallas_call(
        _proj_kernel,
        grid=(t // tm, PROJ_W // PROJ_TN),
        in_specs=[
            pl.BlockSpec((tm, d), lambda i, j: (i, 0)),
            pl.BlockSpec((1, 1, d), lambda i, j: ((i // per_b) * 6 + 0, 0, 0)),
            pl.BlockSpec((1, 1, d), lambda i, j: ((i // per_b) * 6 + 1, 0, 0)),
            _const_spec((1, d)),
            pl.BlockSpec((d, PROJ_TN), lambda i, j: (0, j)),
        ],
        out_specs=pl.BlockSpec((tm, PROJ_TN), lambda i, j: (i, j)),
        out_shape=jax.ShapeDtypeStruct((t, PROJ_W), BF16),
        scratch_shapes=[pltpu.VMEM((tm, d), BF16)],
        compiler_params=_params(("arbitrary", "arbitrary")),
        name="in_proj",
    )(x2, mod3, mod3, g, w)


def _mla_prep_kernel(in_ref, mc_ref, ms_ref, gq_ref, gkv_ref, wa_ref, wb_ref, wk_ref, wv_ref, q_ref, k_ref, v_ref):
    blk = in_ref[...].astype(F32)
    qn = _rms(blk[:, :MLA_Q_RANK], gq_ref[...]).astype(BF16)
    kvn = _rms(blk[:, MLA_Q_RANK:MLA_Q_RANK + MLA_KV_RANK], gkv_ref[...]).astype(BF16)
    kb = blk[:, MLA_Q_RANK + MLA_KV_RANK:]
    mc = mc_ref[...]
    ms = ms_ref[...]
    qa = jnp.dot(qn, wa_ref[...], preferred_element_type=F32)
    qb = jnp.dot(qn, wb_ref[...], preferred_element_type=F32)
    kn = jnp.dot(kvn, wk_ref[...], preferred_element_type=F32)
    lane = lax.broadcasted_iota(jnp.int32, kb.shape, 1)
    rope = (lane >= MLA_NOPE) & (lane < MLA_NOPE + MLA_ROPE)
    kpe = jnp.where(rope, kb * mc + pltpu.roll(kb, LANES // 2, 1) * ms, 0.0)
    scale = (MLA_NOPE + MLA_ROPE) ** -0.5 * LOG2E
    for h in range(MLA_HEADS):
        sl = slice(h * QPAD, (h + 1) * QPAD)
        q_ref[:, sl] = ((qa[:, sl] * mc + qb[:, sl] * ms) * scale).astype(BF16)
        k_ref[:, sl] = (kn[:, sl] + kpe).astype(BF16)
    vt = lax.dot_general(wv_ref[...], kvn, (((1,), (1,)), ((), ())), preferred_element_type=F32)
    row = lax.broadcasted_iota(jnp.int32, vt.shape, 0)
    v_ref[0, 0] = jnp.where(row % QPAD >= MLA_V, 1.0, vt).astype(BF16)


def _mla_prep(proj, mc, ms, gq, gkv, wa, wb, wk, wv, batch, seq):
    t = proj.shape[0]
    tm = min(ATT_TILE, seq)
    per_b = seq // tm
    hq = MLA_HEADS * QPAD
    hv = MLA_HEADS * QPAD
    return pl.pallas_call(
        _mla_prep_kernel,
        grid=(t // tm,),
        in_specs=[
            pl.BlockSpec((tm, MLA_W), lambda i: (i, MLA_OFF // MLA_W)),
            pl.BlockSpec((tm, LANES), lambda i: (i, 0)),
            pl.BlockSpec((tm, LANES), lambda i: (i, 0)),
            _const_spec((1, MLA_Q_RANK)),
            _const_spec((1, MLA_KV_RANK)),
            _const_spec((MLA_Q_RANK, hq)),
            _const_spec((MLA_Q_RANK, hq)),
            _const_spec((MLA_KV_RANK, hq)),
            _const_spec((hv, MLA_KV_RANK)),
        ],
        out_specs=[
            pl.BlockSpec((tm, hq), lambda i: (i, 0)),
            pl.BlockSpec((tm, hq), lambda i: (i, 0)),
            pl.BlockSpec((1, 1, hv, tm), lambda i: (i // per_b, i % per_b, 0, 0)),
        ],
        out_shape=[
            jax.ShapeDtypeStruct((t, hq), BF16),
            jax.ShapeDtypeStruct((t, hq), BF16),
            jax.ShapeDtypeStruct((batch, per_b, hv, tm), BF16),
        ],
        compiler_params=_params(("arbitrary",)),
        name="mla_prep",
    )(proj, mc, ms, gq, gkv, wa, wb, wk, wv)


def _attn_kernel(q_ref, k_ref, vt_ref, o_ref, m_scr, acc_scr, *, tile):
    qi = pl.program_id(2)
    nn_dims = (((1,), (1,)), ((), ()))

    def head_q(a):
        return q_ref[0, :, a * QPAD:(a + 1) * QPAD]

    def head_k(start, size, a):
        return k_ref[0, pl.ds(start, size), a * QPAD:(a + 1) * QPAD]

    def exact_step(j, src, dst, first, masked):
        start = pl.multiple_of(j * tile, tile)
        for a in range(2):
            st = lax.dot_general(head_k(start, tile, a), head_q(a), nn_dims, preferred_element_type=F32)
            if masked:
                key = lax.broadcasted_iota(jnp.int32, (tile, tile), 0)
                qry = lax.broadcasted_iota(jnp.int32, (tile, tile), 1)
                st = jnp.where((key // CHUNK) <= (qry // CHUNK), st, -jnp.inf)
            m_new = jnp.max(st, axis=0, keepdims=True)
            if not first:
                m_old = m_scr[src, a]
                m_new = jnp.maximum(m_old, m_new)
            p = jnp.exp2(st - m_new).astype(BF16)
            vt = vt_ref[0, j, a * QPAD:(a + 1) * QPAD, :]
            pv = jnp.dot(vt, p, preferred_element_type=F32)
            if not first:
                pv = pv + jnp.exp2(m_old - m_new) * acc_scr[src, a]
            acc_scr[dst, a] = pv
            m_scr[dst, a] = m_new

    exact_step(qi, 0, 0, first=True, masked=True)

    def body(j, cur):
        start = pl.multiple_of(j * tile, tile)
        nxt = 1 - cur
        n_sub = tile // ATT_SUB
        scores = {}
        for a in range(2):
            for s in range(n_sub):
                k = head_k(start + s * ATT_SUB, ATT_SUB, a)
                scores[a, s] = lax.dot_general(k, head_q(a), nn_dims, preferred_element_type=F32)
        excess = jnp.zeros((1, tile), F32)
        for a in range(2):
            m_used = m_scr[cur, a]
            m_new = m_used
            pv = jnp.zeros((QPAD, tile), F32)
            for s in range(n_sub):
                st = scores[a, s]
                m_new = jnp.maximum(m_new, jnp.max(st, axis=0, keepdims=True))
                p = jnp.exp2(st - m_used).astype(BF16)
                vt = vt_ref[0, j, a * QPAD:(a + 1) * QPAD, s * ATT_SUB:(s + 1) * ATT_SUB]
                pv = pv + jnp.dot(vt, p, preferred_element_type=F32)
            acc_scr[nxt, a] = (acc_scr[cur, a] + pv) * jnp.exp2(m_used - m_new)
            m_scr[nxt, a] = m_new
            excess = jnp.maximum(excess, m_new - m_used)

        @pl.when(jnp.max(excess) > LAZY_MAX_HEADROOM)
        def _():
            exact_step(j, cur, nxt, first=False, masked=False)

        return nxt

    cur = lax.fori_loop(0, qi, body, 0)

    lane = lax.broadcasted_iota(jnp.int32, (tile, LANES), 1)
    r0 = acc_scr[cur, 0].T
    r1 = acc_scr[cur, 1].T
    o0 = r0 / r0[:, MLA_V:MLA_V + 1]
    o1 = pltpu.roll(r1, MLA_V, 1) / r1[:, MLA_V:MLA_V + 1]
    o_ref[0] = jnp.where(lane < MLA_V, o0, o1).astype(BF16)


def _attention(q, k, vt, batch, seq):
    tile = min(ATT_TILE, seq)
    q3 = q.reshape(batch, seq, MLA_HEADS * QPAD)
    k3 = k.reshape(batch, seq, MLA_HEADS * QPAD)
    out = pl.pallas_call(
        functools.partial(_attn_kernel, tile=tile),
        grid=(batch, MLA_HEADS // 2, seq // tile),
        in_specs=[
            pl.BlockSpec((1, tile, 2 * QPAD), lambda b, h, i: (b, i, h)),
            pl.BlockSpec((1, seq, 2 * QPAD), lambda b, h, i: (b, 0, h)),
            pl.BlockSpec((1, seq // tile, 2 * QPAD, tile), lambda b, h, i: (b, 0, h, 0)),
        ],
        out_specs=pl.BlockSpec((1, tile, 2 * MLA_V), lambda b, h, i: (b, i, h)),
        out_shape=jax.ShapeDtypeStruct((batch, seq, MLA_HEADS * MLA_V), BF16),
        scratch_shapes=[
            pltpu.VMEM((2, 2, 1, tile), F32),
            pltpu.VMEM((2, 2, QPAD, tile), F32),
        ],
        compiler_params=_params(("arbitrary", "arbitrary", "arbitrary")),
        name="mla_attention",
    )(q3, k3, vt)
    return out.reshape(batch * seq, MLA_HEADS * MLA_V)


def _ret_kernel(lg_ref, q_ref, k_ref, v_ref, g_ref, rc_ref, rs_ref, o_ref, state_scr, decay_scr, *, blk):
    n = pl.program_id(1)

    @pl.when(n == 0)
    def _():
        state_scr[...] = jnp.zeros(state_scr.shape, F32)
        row = lax.broadcasted_iota(jnp.int32, (blk, blk), 0)
        col = lax.broadcasted_iota(jnp.int32, (blk, blk), 1)
        dist = jnp.abs(row - col).astype(F32)
        visible = (col // CHUNK) <= (row // CHUNK)
        for h in range(RET_HEADS):
            decay_scr[h] = jnp.where(visible, jnp.exp(lg_ref[h] * dist), 0.0)

    rc = rc_ref[...]
    rs = rs_ref[...]
    half = RET_DK // 2
    pos = lax.broadcasted_iota(jnp.int32, (blk, 1), 0).astype(F32)
    for h in range(RET_HEADS):
        lg = lg_ref[h]
        ksl = slice(h * RET_DK, (h + 1) * RET_DK)
        vsl = slice(h * RET_DV, (h + 1) * RET_DV)
        qf = q_ref[:, ksl].astype(F32)
        kf = k_ref[:, ksl].astype(F32)
        q = qf * rc + pltpu.roll(qf, half, 1) * rs
        k = (kf * rc + pltpu.roll(kf, half, 1) * rs) * (RET_DK ** -0.5)
        v = v_ref[:, vsl]
        q_dec = jnp.exp(lg * (pos + 1.0))
        k_dec = jnp.exp(lg * (blk - 1.0 - pos))

        s = lax.dot_general(q.astype(BF16), k.astype(BF16), (((1,), (1,)), ((), ())), preferred_element_type=F32)
        s = s * decay_scr[h]
        state = state_scr[h]
        o = jnp.dot(s.astype(BF16), v, preferred_element_type=F32)
        o = o + jnp.dot((q * q_dec).astype(BF16), state.astype(BF16), preferred_element_type=F32)
        kd = (k * k_dec).astype(BF16)
        upd = lax.dot_general(kd, v, (((0,), (0,)), ((), ())), preferred_element_type=F32)
        state_scr[h] = state * jnp.exp(lg * blk) + upd

        o = o * lax.rsqrt(jnp.mean(o * o, axis=-1, keepdims=True) + EPS)
        g = g_ref[:, vsl].astype(F32)
        o_ref[:, vsl] = (g * _sigmoid(g) * o).astype(BF16)


def _retention(proj, rc, rs, batch, seq):
    t = proj.shape[0]
    blk = min(RET_BLOCK, seq)
    per_b = seq // blk
    log_gamma = jnp.log(1.0 - 2.0 ** (-5.0 - jnp.arange(RET_HEADS, dtype=F32)))
    qk_w = RET_HEADS * RET_DK
    v_w = RET_HEADS * RET_DV
    row = lambda b, n: b * per_b + n
    return pl.pallas_call(
        functools.partial(_ret_kernel, blk=blk),
        grid=(batch, per_b),
        in_specs=[
            pl.BlockSpec(memory_space=pltpu.SMEM),
            pl.BlockSpec((blk, qk_w), lambda b, n: (row(b, n), RQ_OFF // qk_w)),
            pl.BlockSpec((blk, qk_w), lambda b, n: (row(b, n), RK_OFF // qk_w)),
            pl.BlockSpec((blk, v_w), lambda b, n: (row(b, n), RV_OFF // v_w)),
            pl.BlockSpec((blk, v_w), lambda b, n: (row(b, n), RG_OFF // v_w)),
            pl.BlockSpec((blk, LANES), lambda b, n: (row(b, n), 0)),
            pl.BlockSpec((blk, LANES), lambda b, n: (row(b, n), 0)),
        ],
        out_specs=pl.BlockSpec((blk, v_w), lambda b, n: (row(b, n), 0)),
        out_shape=jax.ShapeDtypeStruct((t, v_w), BF16),
        scratch_shapes=[pltpu.VMEM((RET_HEADS, RET_DK, RET_DV), F32), pltpu.VMEM((RET_HEADS, blk, blk), F32)],
        compiler_params=_params(("arbitrary", "arbitrary")),
        name="retention",
    )(log_gamma, proj, proj, proj, proj, rc, rs)


def _mix_kernel(gate_ref, conv_ref, att_ref, ret_ref, x_ref, gt_ref, sh_ref, sc_ref, gf_ref, cw_ref,
                wom_ref, woc_ref, wor_ref, wmo_ref, wrh_ref, wrl_ref, br_ref,
                xo_ref, h_ref, rout_ref, cnt_ref, u_scr, *, tm, per_b):
    i = pl.program_id(0)
    d = D_MODEL
    cw = CONV_WIDTH

    cb = conv_ref[:, 0:cw].astype(F32)
    u = conv_ref[:, cw:2 * cw].astype(F32) * conv_ref[:, 2 * cw:3 * cw].astype(F32)

    @pl.when(i % per_b == 0)
    def _():
        u_scr[0:SUBLANES, :] = jnp.zeros((SUBLANES, cw), F32)

    @pl.when(i % per_b != 0)
    def _():
        u_scr[0:SUBLANES, :] = u_scr[tm:tm + SUBLANES, :]

    u_scr[SUBLANES:SUBLANES + tm, :] = u
    y = cw_ref[2:3, :] * u
    y = y + cw_ref[1:2, :] * u_scr[SUBLANES - 1:SUBLANES - 1 + tm, :]
    y = y + cw_ref[0:1, :] * u_scr[SUBLANES - 2:SUBLANES - 2 + tm, :]
    z = (cb * y).astype(BF16)

    y_mla = jnp.dot(att_ref[...], wom_ref[...], preferred_element_type=F32)
    y_conv = jnp.dot(z, woc_ref[...], preferred_element_type=F32)
    y_ret = jnp.dot(ret_ref[...], wor_ref[...], preferred_element_type=F32)
    merged = _sigmoid(gate_ref[:, 0:d].astype(F32)) * y_mla
    merged = merged + _sigmoid(gate_ref[:, d:2 * d].astype(F32)) * y_conv
    merged = merged + _sigmoid(gate_ref[:, 2 * d:3 * d].astype(F32)) * y_ret
    x = x_ref[...] + gt_ref[0] * jnp.dot(merged.astype(BF16), wmo_ref[...], preferred_element_type=F32)
    xo_ref[...] = x

    h = _rms(x, gf_ref[...]) * (1.0 + sc_ref[0]) + sh_ref[0]
    _store_token_tiles(h_ref, 0, tm, h)

    h_hi = h.astype(BF16)
    h_lo = (h - h_hi.astype(F32)).astype(BF16)
    lg = jnp.dot(h_hi, wrh_ref[...], preferred_element_type=F32)
    lg = lg + jnp.dot(h_lo, wrh_ref[...], preferred_element_type=F32)
    lg = lg + jnp.dot(h_hi, wrl_ref[...], preferred_element_type=F32)
    lg = lg + br_ref[...]

    lane = lax.broadcasted_iota(jnp.int32, lg.shape, 1)
    neg = -jnp.inf
    gl = jnp.where(lane < N_GROUPS, lg, neg)
    gmax = jnp.max(gl, axis=-1, keepdims=True)
    g_val = 1.0 / jnp.sum(jnp.exp(gl - gmax), axis=-1, keepdims=True)
    g_idx = jnp.min(jnp.where(gl == gmax, lane, LANES), axis=-1, keepdims=True)
    in_group = (lane >= ROUTE_E_OFF) & (lane < ROUTE_E_OFF + N_EXPERTS)
    in_group = in_group & (((lane - ROUTE_E_OFF) // EXPERTS_PER_GROUP) == g_idx)
    el = jnp.where(in_group, lg, neg)
    m1 = jnp.max(el, axis=-1, keepdims=True)
    i1 = jnp.min(jnp.where(el == m1, lane, LANES), axis=-1, keepdims=True)
    el2 = jnp.where(lane == i1, neg, el)
    m2 = jnp.max(el2, axis=-1, keepdims=True)
    i2 = jnp.min(jnp.where(el2 == m2, lane, LANES), axis=-1, keepdims=True)
    r = jnp.exp(m2 - m1)
    w1 = g_val / (1.0 + r)
    w2 = g_val * r / (1.0 + r)
    e1 = (i1 - ROUTE_E_OFF).astype(F32)
    e2 = (i2 - ROUTE_E_OFF).astype(F32)
    rout = jnp.where(lane == 0, e1, jnp.where(lane == 1, e2, jnp.where(lane == 2, w1, jnp.where(lane == 3, w2, 0.0))))
    rout_ref[...] = rout.T[0:SUBLANES, :]

    hits = jnp.where(lane == i1, 1.0, 0.0) + jnp.where(lane == i2, 1.0, 0.0)

    @pl.when(i == 0)
    def _():
        cnt_ref[...] = jnp.zeros(cnt_ref.shape, F32)

    cnt_ref[0:1, :] = cnt_ref[0:1, :] + jnp.sum(hits, axis=0, keepdims=True)


def _mix(proj, att, ret, x2, mod3, gf, cw, wom, woc, wor, wmo, wrh, wrl, br, seq):
    t, d = x2.shape
    tm = min(ROW_TILE, seq)
    per_b = seq // tm
    row = lambda i: (i, 0)
    modspec = lambda k: pl.BlockSpec((1, 1, d), lambda i: ((i // per_b) * 6 + k, 0, 0))
    return pl.pallas_call(
        functools.partial(_mix_kernel, tm=tm, per_b=per_b),
        grid=(t // tm,),
        in_specs=[
            pl.BlockSpec((tm, GATE_W), row),
            pl.BlockSpec((tm, 3 * CONV_WIDTH), lambda i: (i, CONV_OFF // (3 * CONV_WIDTH))),
            pl.BlockSpec((tm, MLA_HEADS * MLA_V), row),
            pl.BlockSpec((tm, RET_HEADS * RET_DV), row),
            pl.BlockSpec((tm, d), row),
            modspec(2), modspec(3), modspec(4),
            _const_spec((1, d)),
            _const_spec((SUBLANES, CONV_WIDTH)),
            _const_spec(wom.shape), _const_spec(woc.shape), _const_spec(wor.shape), _const_spec(wmo.shape),
            _const_spec(wrh.shape), _const_spec(wrl.shape), _const_spec((1, LANES)),
        ],
        out_specs=[pl.BlockSpec((tm, d), row), pl.BlockSpec((tm * ROW_SPLIT, LANES), row),
                   pl.BlockSpec((SUBLANES, tm), lambda i: (0, i)), _const_spec((SUBLANES, LANES))],
        out_shape=[
            jax.ShapeDtypeStruct((t, d), F32),
            jax.ShapeDtypeStruct((t * ROW_SPLIT, LANES), F32),
            jax.ShapeDtypeStruct((SUBLANES, t), F32),
            jax.ShapeDtypeStruct((SUBLANES, LANES), F32),
        ],
        scratch_shapes=[pltpu.VMEM((tm + SUBLANES, CONV_WIDTH), F32)],
        compiler_params=_params(("arbitrary",)),
        name="mix_router",
    )(proj, proj, att, ret, x2, mod3, mod3, mod3, gf, cw, wom, woc, wor, wmo, wrh, wrl, br)


def _moe_kernel(te_ref, nv_ref, meta_ref, meta_next_ref, wt_ref, wg_ref, wu_ref, wd_ref, h_hbm, out_hbm,
                xbuf, ybuf, wgb, wub, wdb, gsem, ssem, *, tm):
    i = pl.program_id(0)
    nt = pl.num_programs(0)
    slot = i % 2
    nv = nv_ref[i]
    prev = jnp.maximum(i - 1, 0)
    prev2 = jnp.maximum(i - 2, 0)

    def gather_copy(tok, s, r):
        src = h_hbm.at[pl.ds(pl.multiple_of(tok * ROW_SPLIT, ROW_SPLIT), ROW_SPLIT)]
        return pltpu.make_async_copy(src, xbuf.at[pl.ds((s * tm + r) * ROW_SPLIT, ROW_SPLIT)], gsem.at[s])

    def scatter_copy(dst, s, r):
        dst_rows = out_hbm.at[pl.ds(pl.multiple_of(dst * ROW_SPLIT, ROW_SPLIT), ROW_SPLIT)]
        return pltpu.make_async_copy(ybuf.at[pl.ds((s * tm + r) * ROW_SPLIT, ROW_SPLIT)], dst_rows, ssem.at[s])

    def start_gather(meta, s):
        for r in range(tm):
            gather_copy(meta[0, 0, r], s, r).start(priority=r % 2)

    def wait_gather(s):
        for r in range(tm):
            gather_copy(0, s, r).wait()

    def start_scatter(s):
        for r in range(tm):
            scatter_copy(meta_ref[0, 0, tm + r], s, r).start(priority=r % 2)

    def wait_scatter(s):
        for r in range(tm):
            scatter_copy(0, s, r).wait()

    def for_slot(fn):
        @pl.when(slot == 0)
        def _():
            fn(0)

        @pl.when(slot == 1)
        def _():
            fn(1)

    @pl.when(i == 0)
    def _():
        ybuf[...] = jnp.zeros(ybuf.shape, F32)
        half = tm * ROW_SPLIT
        for s in range(2):
            pad_rows = out_hbm.at[pl.ds(out_hbm.shape[0] - (2 - s) * half, half)]
            init = pltpu.make_async_copy(ybuf.at[pl.ds(s * half, half)], pad_rows, ssem.at[s])
            init.start()
            init.wait()
        start_gather(meta_ref, 0)

    @pl.when((i >= 2) & (nv_ref[prev2] > 0))
    def _():
        for_slot(wait_scatter)

    @pl.when((i == 0) | (te_ref[i] != te_ref[prev]))
    def _():
        wgb[...] = wg_ref[0].astype(BF16)
        wub[...] = wu_ref[0].astype(BF16)
        wdb[...] = wd_ref[0].astype(BF16)

    def active(s):
        wait_gather(s)
        x = _load_token_tiles(xbuf, s * tm * ROW_SPLIT, tm).astype(BF16)
        start_gather(meta_next_ref, 1 - s)
        g = jnp.dot(x, wgb[...], preferred_element_type=F32)
        u = jnp.dot(x, wub[...], preferred_element_type=F32)
        hid = (g * _sigmoid(g) * u).astype(BF16)
        y = jnp.dot(hid, wdb[...], preferred_element_type=F32) * wt_ref[0]
        _store_token_tiles(ybuf, s * tm * ROW_SPLIT, tm, y)
        start_scatter(s)

    @pl.when(nv > 0)
    def _():
        for_slot(active)

    @pl.when((i > 0) & (nv == 0) & (nv_ref[prev] > 0))
    def _():
        for_slot(wait_gather)

    @pl.when((i == nt - 1) & (nv_ref[prev] > 0))
    def _():
        for_slot(lambda s: wait_scatter(1 - s))


def _route_tables(rout, cnt, t, tm, nt):
    e = rout[0:2].astype(jnp.int32).reshape(-1)
    w = rout[2:4].reshape(-1)
    n_slot = 2 * t
    perm = jnp.argsort(e).astype(jnp.int32)
    counts = cnt[0, ROUTE_E_OFF:ROUTE_E_OFF + N_EXPERTS].astype(jnp.int32)
    nt_e = (counts + tm - 1) // tm
    tile_end = jnp.cumsum(nt_e)
    tile_start = tile_end - nt_e
    off = jnp.cumsum(counts) - counts
    total = tile_end[-1]
    i = jnp.arange(nt, dtype=jnp.int32)
    e_raw = jnp.minimum(jnp.sum((tile_end[None, :] <= i[:, None]).astype(jnp.int32), axis=1), N_EXPERTS - 1)
    e_i = jnp.where(i < total, e_raw, e_raw[jnp.maximum(total - 1, 0)])
    j = i - tile_start[e_i]
    nvalid = jnp.where(i < total, jnp.clip(counts[e_i] - j * tm, 0, tm), 0).astype(jnp.int32)
    r = jnp.arange(tm, dtype=jnp.int32)
    pos = jnp.clip((off[e_i] + j * tm)[:, None] + r[None, :], 0, n_slot - 1)
    slot = perm[pos]
    valid = r[None, :] < nvalid[:, None]
    tok = jnp.where(slot >= t, slot - t, slot)
    dst = jnp.where(valid, slot, n_slot + (i % 2)[:, None] * tm + r[None, :])
    wt = jnp.where(valid, w[slot], 0.0)
    meta = jnp.concatenate([tok, dst], axis=1).astype(jnp.int32).reshape(nt, 1, 2 * tm)
    return e_i, nvalid, meta, wt.reshape(nt, tm, 1)


def _moe(h, rout, cnt, wg, wu, wd, layer):
    t, d = h.shape[0] // ROW_SPLIT, D_MODEL
    tm = min(MOE_TILE, t)
    nt = (2 * t) // tm + N_EXPERTS
    e_i, nvalid, meta, wt = _route_tables(rout, cnt, t, tm, nt)
    f = wg.shape[-1]
    grid_spec = pltpu.PrefetchScalarGridSpec(
        num_scalar_prefetch=2,
        grid=(nt,),
        in_specs=[
            pl.BlockSpec((1, 1, 2 * tm), lambda i, te, nv: (i, 0, 0), memory_space=pltpu.SMEM),
            pl.BlockSpec((1, 1, 2 * tm), lambda i, te, nv: (jnp.minimum(i + 1, nt - 1), 0, 0), memory_space=pltpu.SMEM),
            pl.BlockSpec((1, tm, 1), lambda i, te, nv: (i, 0, 0)),
            pl.BlockSpec((1, 1, d, f), lambda i, te, nv: (layer, te[i], 0, 0)),
            pl.BlockSpec((1, 1, d, f), lambda i, te, nv: (layer, te[i], 0, 0)),
            pl.BlockSpec((1, 1, f, d), lambda i, te, nv: (layer, te[i], 0, 0)),
            pl.BlockSpec(memory_space=pl.ANY),
        ],
        out_specs=pl.BlockSpec(memory_space=pl.ANY),
        scratch_shapes=[
            pltpu.VMEM((2 * tm * ROW_SPLIT, LANES), F32),
            pltpu.VMEM((2 * tm * ROW_SPLIT, LANES), F32),
            pltpu.VMEM((d, f), BF16),
            pltpu.VMEM((d, f), BF16),
            pltpu.VMEM((f, d), BF16),
            pltpu.SemaphoreType.DMA((2,)),
            pltpu.SemaphoreType.DMA((2,)),
        ],
    )

    def kern(te_ref, nv_ref, meta_ref, meta_next_ref, wt_ref, wg_ref, wu_ref, wd_ref, h_hbm, out_hbm, *scratch):
        _moe_kernel(te_ref, nv_ref, meta_ref, meta_next_ref, wt_ref, wg_ref.at[0], wu_ref.at[0], wd_ref.at[0],
                    h_hbm, out_hbm, *scratch, tm=tm)

    return pl.pallas_call(
        kern,
        grid_spec=grid_spec,
        out_shape=jax.ShapeDtypeStruct(((2 * t + 2 * tm) * ROW_SPLIT, LANES), F32),
        compiler_params=_params(("arbitrary",)),
        name="moe_experts",
    )(e_i, nvalid, meta, meta, wt, wg, wu, wd, h)


def _combine_kernel(x_ref, a_ref, b_ref, gt_ref, g_ref, o_ref, *, final):
    tm = x_ref.shape[0]
    moe = _load_token_tiles(a_ref, 0, tm) + _load_token_tiles(b_ref, 0, tm)
    x = x_ref[...] + gt_ref[0] * moe
    if final:
        x = _rms(x, g_ref[...])
    o_ref[...] = x


def _combine(x2, moe_out, mod3, final_g, seq, final):
    t, d = x2.shape
    tm = min(512, seq)
    per_b = seq // tm
    return pl.pallas_call(
        functools.partial(_combine_kernel, final=final),
        grid=(t // tm,),
        in_specs=[
            pl.BlockSpec((tm, d), lambda i: (i, 0)),
            pl.BlockSpec((tm * ROW_SPLIT, LANES), lambda i: (i, 0)),
            pl.BlockSpec((tm * ROW_SPLIT, LANES), lambda i: (t // tm + i, 0)),
            pl.BlockSpec((1, 1, d), lambda i: ((i // per_b) * 6 + 5, 0, 0)),
            _const_spec((1, d)),
        ],
        out_specs=pl.BlockSpec((tm, d), lambda i: (i, 0)),
        out_shape=jax.ShapeDtypeStruct((t, d), F32),
        compiler_params=_params(("arbitrary",)),
        name="combine",
    )(x2, moe_out, moe_out, mod3, final_g)


def _prep_w_in(w):
    d = w.shape[0]
    sizes = (MLA_Q_RANK, MLA_KV_RANK, MLA_ROPE, CONV_WIDTH, CONV_WIDTH, CONV_WIDTH,
             RET_HEADS * RET_DK, RET_HEADS * RET_DK, RET_HEADS * RET_DV, RET_HEADS * RET_DV, 3 * D_MODEL)
    parts = []
    start = 0
    for size in sizes:
        parts.append(w[:, start:start + size])
        start += size
    q_lat, kv_lat, kr, cb, cc, cx, rq, rk, rv, rg, gl = parts
    half = MLA_ROPE // 2
    zeros = lambda n: jnp.zeros((d, n), w.dtype)
    kblock = jnp.concatenate(
        [-kr[:, half:], kr[:, :half], zeros(MLA_NOPE - MLA_ROPE), kr, zeros(LANES - MLA_NOPE - MLA_ROPE)], axis=1)
    out = jnp.concatenate([gl, rq, rk, rv, rg, cb, cc, cx, q_lat, kv_lat, kblock], axis=1)
    return out.astype(BF16)


def _prep_w_uq(w):
    r = w.shape[0]
    dq = MLA_NOPE + MLA_ROPE
    half = MLA_ROPE // 2
    w3 = w.reshape(r, MLA_HEADS, dq)
    pad = QPAD - dq
    wa = jnp.pad(w3, ((0, 0), (0, 0), (0, pad))).reshape(r, MLA_HEADS * QPAD)
    wb = jnp.concatenate(
        [jnp.zeros((r, MLA_HEADS, MLA_NOPE), w.dtype), -w3[:, :, MLA_NOPE + half:], w3[:, :, MLA_NOPE:MLA_NOPE + half],
         jnp.zeros((r, MLA_HEADS, pad), w.dtype)], axis=2).reshape(r, MLA_HEADS * QPAD)
    return wa.astype(BF16), wb.astype(BF16)


def _prep_w_ukv(w):
    r = w.shape[0]
    w3 = w.reshape(r, MLA_HEADS, MLA_NOPE + MLA_V)
    wk = jnp.pad(w3[:, :, :MLA_NOPE], ((0, 0), (0, 0), (0, QPAD - MLA_NOPE))).reshape(r, MLA_HEADS * QPAD)
    wv = jnp.pad(w3[:, :, MLA_NOPE:], ((0, 0), (0, 0), (0, QPAD - MLA_V))).reshape(r, MLA_HEADS * QPAD)
    return wk.astype(BF16), wv.T.astype(BF16)


def _prep_router(w_rg, b_rg, w_re, b_re):
    d = w_rg.shape[0]
    w = jnp.zeros((d, LANES), F32)
    w = w.at[:, :N_GROUPS].set(w_rg).at[:, ROUTE_E_OFF:ROUTE_E_OFF + N_EXPERTS].set(w_re)
    b = jnp.zeros((1, LANES), F32)
    b = b.at[0, :N_GROUPS].set(b_rg).at[0, ROUTE_E_OFF:ROUTE_E_OFF + N_EXPERTS].set(b_re)
    w_hi = w.astype(BF16)
    w_lo = (w - w_hi.astype(F32)).astype(BF16)
    return w_hi, w_lo, b


def kernel(x, c, positions, w_ada, b_ada, norm_mix_g, norm_ffn_g, w_in, mla_q_norm_g, mla_kv_norm_g, w_uq, w_ukv, w_o_mla, conv_w, w_o_conv, w_o_ret, w_mix_out, w_route_group, b_route_group, w_route_expert, b_route_expert, w_exp_gate, w_exp_up, w_exp_down, final_g):
    batch, seq, d = x.shape
    depth = w_in.shape[0]
    t = batch * seq
    x2 = x.reshape(t, d)
    mod = _ada_mod(c, w_ada, b_ada)
    rc, rs, mc, ms = _rope_tables(positions)
    for l in range(depth):
        mod3 = mod[l].reshape(batch * 6, 1, d)
        proj = _in_proj(x2, mod3, norm_mix_g[l].reshape(1, d), _prep_w_in(w_in[l]), seq)
        wa, wb = _prep_w_uq(w_uq[l])
        wk, wv = _prep_w_ukv(w_ukv[l])
        q, k, v = _mla_prep(proj, mc, ms, mla_q_norm_g[l].reshape(1, -1), mla_kv_norm_g[l].reshape(1, -1), wa, wb, wk, wv,
                            batch, seq)
        att = _attention(q, k, v, batch, seq)
        ret = _retention(proj, rc, rs, batch, seq)
        wrh, wrl, br = _prep_router(w_route_group[l], b_route_group[l], w_route_expert[l], b_route_expert[l])
        cw = jnp.pad(conv_w[l], ((0, SUBLANES - CONV_K), (0, 0)))
        x2, h, rout, cnt = _mix(proj, att, ret, x2, mod3, norm_ffn_g[l].reshape(1, d), cw,
                           w_o_mla[l].astype(BF16), w_o_conv[l].astype(BF16), w_o_ret[l].astype(BF16),
                           w_mix_out[l].astype(BF16), wrh, wrl, br, seq)
        moe_out = _moe(h, rout, cnt, w_exp_gate, w_exp_up, w_exp_down, l)
        x2 = _combine(x2, moe_out, mod3, final_g.reshape(1, d), seq, final=(l == depth - 1))
    return x2.reshape(batch, seq, d)
```

```python
import functools

import jax
import jax.numpy as jnp
from jax import lax
from jax.experimental import pallas as pl
from jax.experimental.pallas import tpu as pltpu

F32 = jnp.float32
BF16 = jnp.bfloat16

D_MODEL = 1024
CHUNK = 64
EPS = 1e-6
ROPE_THETA = 10000.0
LOG2E = 1.4426950408889634

MLA_HEADS = 8
MLA_Q_RANK = 384
MLA_KV_RANK = 256
MLA_NOPE = 64
MLA_ROPE = 32
MLA_V = 64

CONV_WIDTH = 512
CONV_K = 3

RET_HEADS = 4
RET_DK = 128
RET_DV = 256

N_GROUPS = 4
EXPERTS_PER_GROUP = 8
N_EXPERTS = N_GROUPS * EXPERTS_PER_GROUP
EXPERT_FF = 512

LANES = 128
SUBLANES = 8
VMEM_LIMIT = 52 * 1024 * 1024

GATE_W = 3 * D_MODEL
RQ_OFF = GATE_W
RK_OFF = RQ_OFF + RET_HEADS * RET_DK
RV_OFF = RK_OFF + RET_HEADS * RET_DK
RG_OFF = RV_OFF + RET_HEADS * RET_DV
CONV_OFF = RG_OFF + RET_HEADS * RET_DV
MLA_OFF = CONV_OFF + 3 * CONV_WIDTH
MLA_W = MLA_Q_RANK + MLA_KV_RANK + LANES
PROJ_W = MLA_OFF + MLA_W
PROJ_TN = PROJ_W // 6
assert PROJ_TN * 6 == PROJ_W and PROJ_TN % LANES == 0
assert MLA_OFF % MLA_W == 0 and CONV_OFF % (3 * CONV_WIDTH) == 0

QPAD = LANES
ROUTE_E_OFF = 32

ROW_TILE_PROJ = 1024
ROW_TILE = 256
ATT_TILE = 512
ATT_SUB = 256
ATT_GROUP = 4
LAZY_MAX_HEADROOM = 60.0
RET_BLOCK = 256
MOE_TILE = 256
ADA_TN = 768


def _params(sem, vmem=VMEM_LIMIT):
    return pltpu.CompilerParams(dimension_semantics=sem, vmem_limit_bytes=vmem)


def _const_spec(shape):
    nd = len(shape)
    return pl.BlockSpec(shape, lambda *_: (0,) * nd)


ROW_SPLIT = D_MODEL // LANES
assert ROW_SPLIT == SUBLANES


def _store_token_tiles(ref, base, rows, val):
    for s in range(ROW_SPLIT):
        ref[pl.ds(base + s, rows, stride=ROW_SPLIT), :] = val[:, s * LANES:(s + 1) * LANES]


def _load_token_tiles(ref, base, rows):
    return jnp.concatenate([ref[pl.ds(base + s, rows, stride=ROW_SPLIT), :] for s in range(ROW_SPLIT)], axis=1)


def _rms(x, g):
    return x * lax.rsqrt(jnp.mean(x * x, axis=-1, keepdims=True) + EPS) * g


def _sigmoid(x):
    return 1.0 / (1.0 + jnp.exp(-x))


def _ada_kernel(ct_ref, w_ref, b_ref, o_ref):
    ct = ct_ref[...]
    a = ct * _sigmoid(ct)
    w = w_ref[0]
    rows = [jnp.sum(w * a[:, b:b + 1], axis=0, keepdims=True) for b in range(ct.shape[1])]
    o_ref[0] = jnp.concatenate(rows, axis=0) + b_ref[0]


def _ada_mod(c, w_ada, b_ada):
    depth, d, n = w_ada.shape
    b = c.shape[0]
    return pl.pallas_call(
        _ada_kernel,
        grid=(depth, n // ADA_TN),
        in_specs=[
            _const_spec((d, b)),
            pl.BlockSpec((1, d, ADA_TN), lambda l, j: (l, 0, j)),
            pl.BlockSpec((1, 1, ADA_TN), lambda l, j: (l, 0, j)),
        ],
        out_specs=pl.BlockSpec((1, b, ADA_TN), lambda l, j: (l, 0, j)),
        out_shape=jax.ShapeDtypeStruct((depth, b, n), F32),
        compiler_params=_params(("arbitrary", "arbitrary")),
        name="ada_mod",
    )(c.T, w_ada, b_ada.reshape(depth, 1, n))


def _rope_kernel(pos_ref, f_ref, rc_ref, rs_ref, mc_ref, ms_ref):
    ang = pos_ref[...].astype(F32) * f_ref[...]
    cs = jnp.cos(ang)
    sn = jnp.sin(ang)
    lane = lax.broadcasted_iota(jnp.int32, ang.shape, 1)
    half = LANES // 2
    lo = lane < half
    rc_ref[...] = jnp.where(lo, cs, pltpu.roll(cs, half, 1))
    rs_ref[...] = jnp.where(lo, -sn, pltpu.roll(sn, half, 1))
    rope = (lane >= MLA_NOPE) & (lane < MLA_NOPE + MLA_ROPE)
    mc_ref[...] = jnp.where(lane < MLA_NOPE, 1.0, jnp.where(rope, cs, 0.0))
    ms_ref[...] = jnp.where(rope, sn, 0.0)


def _rope_tables(positions):
    t = positions.size
    tm = min(2048, t)
    inv_ret = ROPE_THETA ** (-jnp.arange(0, RET_DK, 2, dtype=F32) / RET_DK)
    inv_mla = ROPE_THETA ** (-jnp.arange(0, MLA_ROPE, 2, dtype=F32) / MLA_ROPE)
    freqs = jnp.concatenate([inv_ret, inv_mla, inv_mla, jnp.zeros((LANES - 96,), F32)]).reshape(1, LANES)
    spec = pl.BlockSpec((tm, LANES), lambda i: (i, 0))
    return pl.pallas_call(
        _rope_kernel,
        grid=(t // tm,),
        in_specs=[pl.BlockSpec((tm, 1), lambda i: (i, 0)), _const_spec((1, LANES))],
        out_specs=[spec] * 4,
        out_shape=[jax.ShapeDtypeStruct((t, LANES), F32)] * 4,
        compiler_params=_params(("arbitrary",)),
        name="rope_tables",
    )(positions.reshape(t, 1), freqs)


def _proj_kernel(x_ref, sh_ref, sc_ref, g_ref, w_ref, o_ref, h_scr):
    @pl.when(pl.program_id(1) == 0)
    def _():
        h = _rms(x_ref[...], g_ref[...]) * (1.0 + sc_ref[0]) + sh_ref[0]
        h_scr[...] = h.astype(BF16)

    o_ref[...] = jnp.dot(h_scr[...], w_ref[...], preferred_element_type=F32).astype(BF16)


def _in_proj(x2, mod3, g, w, seq):
    t, d = x2.shape
    tm = min(ROW_TILE_PROJ, seq)
    per_b = seq // tm
    return pl.pallas_call(
        _proj_kernel,
        grid=(t // tm, PROJ_W // PROJ_TN),
        in_specs=[
            pl.BlockSpec((tm, d), lambda i, j: (i, 0)),
            pl.BlockSpec((1, 1, d), lambda i, j: ((i // per_b) * 6 + 0, 0, 0)),
            pl.BlockSpec((1, 1, d), lambda i, j: ((i // per_b) * 6 + 1, 0, 0)),
            _const_spec((1, d)),
            pl.BlockSpec((d, PROJ_TN), lambda i, j: (0, j)),
        ],
        out_specs=pl.BlockSpec((tm, PROJ_TN), lambda i, j: (i, j)),
        out_shape=jax.ShapeDtypeStruct((t, PROJ_W), BF16),
        scratch_shapes=[pltpu.VMEM((tm, d), BF16)],
        compiler_params=_params(("arbitrary", "arbitrary")),
        name="in_proj",
    )(x2, mod3, mod3, g, w)


def _mla_prep_kernel(in_ref, mc_ref, ms_ref, gq_ref, gkv_ref, wa_ref, wb_ref, wk_ref, wv_ref, q_ref, k_ref, v_ref):
    blk = in_ref[...].astype(F32)
    qn = _rms(blk[:, :MLA_Q_RANK], gq_ref[...]).astype(BF16)
    kvn = _rms(blk[:, MLA_Q_RANK:MLA_Q_RANK + MLA_KV_RANK], gkv_ref[...]).astype(BF16)
    kb = blk[:, MLA_Q_RANK + MLA_KV_RANK:]
    mc = mc_ref[...]
    ms = ms_ref[...]
    qa = jnp.dot(qn, wa_ref[...], preferred_element_type=F32)
    qb = jnp.dot(qn, wb_ref[...], preferred_element_type=F32)
    kn = jnp.dot(kvn, wk_ref[...], preferred_element_type=F32)
    lane = lax.broadcasted_iota(jnp.int32, kb.shape, 1)
    rope = (lane >= MLA_NOPE) & (lane < MLA_NOPE + MLA_ROPE)
    kpe = jnp.where(rope, kb * mc + pltpu.roll(kb, LANES // 2, 1) * ms, 0.0)
    scale = (MLA_NOPE + MLA_ROPE) ** -0.5 * LOG2E
    for h in range(MLA_HEADS):
        sl = slice(h * QPAD, (h + 1) * QPAD)
        q_ref[:, sl] = ((qa[:, sl] * mc + qb[:, sl] * ms) * scale).astype(BF16)
        k_ref[:, sl] = (kn[:, sl] + kpe).astype(BF16)
    vt = lax.dot_general(wv_ref[...], kvn, (((1,), (1,)), ((), ())), preferred_element_type=F32)
    row = lax.broadcasted_iota(jnp.int32, vt.shape, 0)
    v_ref[0, 0] = jnp.where(row % QPAD >= MLA_V, 1.0, vt).astype(BF16)


def _mla_prep(proj, mc, ms, gq, gkv, wa, wb, wk, wv, batch, seq):
    t = proj.shape[0]
    tm = min(ATT_TILE, seq)
    per_b = seq // tm
    hq = MLA_HEADS * QPAD
    hv = MLA_HEADS * QPAD
    return pl.pallas_call(
        _mla_prep_kernel,
        grid=(t // tm,),
        in_specs=[
            pl.BlockSpec((tm, MLA_W), lambda i: (i, MLA_OFF // MLA_W)),
            pl.BlockSpec((tm, LANES), lambda i: (i, 0)),
            pl.BlockSpec((tm, LANES), lambda i: (i, 0)),
            _const_spec((1, MLA_Q_RANK)),
            _const_spec((1, MLA_KV_RANK)),
            _const_spec((MLA_Q_RANK, hq)),
            _const_spec((MLA_Q_RANK, hq)),
            _const_spec((MLA_KV_RANK, hq)),
            _const_spec((hv, MLA_KV_RANK)),
        ],
        out_specs=[
            pl.BlockSpec((tm, hq), lambda i: (i, 0)),
            pl.BlockSpec((tm, hq), lambda i: (i, 0)),
            pl.BlockSpec((1, 1, hv, tm), lambda i: (i // per_b, i % per_b, 0, 0)),
        ],
        out_shape=[
            jax.ShapeDtypeStruct((t, hq), BF16),
            jax.ShapeDtypeStruct((t, hq), BF16),
            jax.ShapeDtypeStruct((batch, per_b, hv, tm), BF16),
        ],
        compiler_params=_params(("arbitrary",)),
        name="mla_prep",
    )(proj, mc, ms, gq, gkv, wa, wb, wk, wv)


def _attn_kernel(q_ref, k_ref, vt_ref, o_ref, m_scr, acc_scr, *, tile):
    qi = pl.program_id(2)
    nn_dims = (((1,), (1,)), ((), ()))

    def head_q(a):
        return q_ref[0, :, a * QPAD:(a + 1) * QPAD]

    def head_k(start, size, a):
        return k_ref[0, pl.ds(start, size), a * QPAD:(a + 1) * QPAD]

    def exact_step(j, src, dst, first, masked):
        start = pl.multiple_of(j * tile, tile)
        for a in range(2):
            st = lax.dot_general(head_k(start, tile, a), head_q(a), nn_dims, preferred_element_type=F32)
            if masked:
                key = lax.broadcasted_iota(jnp.int32, (tile, tile), 0)
                qry = lax.broadcasted_iota(jnp.int32, (tile, tile), 1)
                st = jnp.where((key // CHUNK) <= (qry // CHUNK), st, -jnp.inf)
            m_new = jnp.max(st, axis=0, keepdims=True)
            if not first:
                m_old = m_scr[src, a]
                m_new = jnp.maximum(m_old, m_new)
            p = jnp.exp2(st - m_new).astype(BF16)
            vt = vt_ref[0, j, a * QPAD:(a + 1) * QPAD, :]
            pv = jnp.dot(vt, p, preferred_element_type=F32)
            if not first:
                pv = pv + jnp.exp2(m_old - m_new) * acc_scr[src, a]
            acc_scr[dst, a] = pv
            m_scr[dst, a] = m_new

    exact_step(qi, 0, 0, first=True, masked=True)

    def lazy_step(j0, n_tiles, cur):
        nxt = 1 - cur
        n_sub = tile // ATT_SUB
        scores = {}
        for a in range(2):
            for t in range(n_tiles):
                start = pl.multiple_of((j0 + t) * tile, tile)
                for s in range(n_sub):
                    k = head_k(start + s * ATT_SUB, ATT_SUB, a)
                    scores[a, t, s] = lax.dot_general(k, head_q(a), nn_dims, preferred_element_type=F32)
        excess = jnp.zeros((1, tile), F32)
        for a in range(2):
            m_used = m_scr[cur, a]
            m_new = m_used
            pv = jnp.zeros((QPAD, tile), F32)
            for t in range(n_tiles):
                for s in range(n_sub):
                    st = scores[a, t, s]
                    m_new = jnp.maximum(m_new, jnp.max(st, axis=0, keepdims=True))
                    p = jnp.exp2(st - m_used).astype(BF16)
                    vt = vt_ref[0, j0 + t, a * QPAD:(a + 1) * QPAD, s * ATT_SUB:(s + 1) * ATT_SUB]
                    pv = pv + jnp.dot(vt, p, preferred_element_type=F32)
            acc_scr[nxt, a] = (acc_scr[cur, a] + pv) * jnp.exp2(m_used - m_new)
            m_scr[nxt, a] = m_new
            excess = jnp.maximum(excess, m_new - m_used)

        @pl.when(jnp.max(excess) > LAZY_MAX_HEADROOM)
        def _():
            def redo(t, carry):
                exact_step(j0 + t, jnp.where(t == 0, cur, nxt), nxt, first=False, masked=False)
                return carry

            lax.fori_loop(0, n_tiles, redo, 0)

        return nxt

    n_groups = qi // ATT_GROUP
    cur = lax.fori_loop(0, n_groups, lambda g, c: lazy_step(g * ATT_GROUP, ATT_GROUP, c), 0)
    cur = lax.fori_loop(n_groups * ATT_GROUP, qi, lambda j, c: lazy_step(j, 1, c), cur)

    lane = lax.broadcasted_iota(jnp.int32, (tile, LANES), 1)
    r0 = acc_scr[cur, 0].T
    r1 = acc_scr[cur, 1].T
    o0 = r0 / r0[:, MLA_V:MLA_V + 1]
    o1 = pltpu.roll(r1, MLA_V, 1) / r1[:, MLA_V:MLA_V + 1]
    o_ref[0] = jnp.where(lane < MLA_V, o0, o1).astype(BF16)


def _attention(q, k, vt, batch, seq):
    tile = min(ATT_TILE, seq)
    q3 = q.reshape(batch, seq, MLA_HEADS * QPAD)
    k3 = k.reshape(batch, seq, MLA_HEADS * QPAD)
    out = pl.pallas_call(
        functools.partial(_attn_kernel, tile=tile),
        grid=(batch, MLA_HEADS // 2, seq // tile),
        in_specs=[
            pl.BlockSpec((1, tile, 2 * QPAD), lambda b, h, i: (b, i, h)),
            pl.BlockSpec((1, seq, 2 * QPAD), lambda b, h, i: (b, 0, h)),
            pl.BlockSpec((1, seq // tile, 2 * QPAD, tile), lambda b, h, i: (b, 0, h, 0)),
        ],
        out_specs=pl.BlockSpec((1, tile, 2 * MLA_V), lambda b, h, i: (b, i, h)),
        out_shape=jax.ShapeDtypeStruct((batch, seq, MLA_HEADS * MLA_V), BF16),
        scratch_shapes=[
            pltpu.VMEM((2, 2, 1, tile), F32),
            pltpu.VMEM((2, 2, QPAD, tile), F32),
        ],
        compiler_params=_params(("arbitrary", "arbitrary", "arbitrary")),
        name="mla_attention",
    )(q3, k3, vt)
    return out.reshape(batch * seq, MLA_HEADS * MLA_V)


def _ret_kernel(lg_ref, q_ref, k_ref, v_ref, g_ref, rc_ref, rs_ref, o_ref, state_scr, decay_scr, *, blk):
    n = pl.program_id(1)

    @pl.when(n == 0)
    def _():
        state_scr[...] = jnp.zeros(state_scr.shape, F32)
        row = lax.broadcasted_iota(jnp.int32, (blk, blk), 0)
        col = lax.broadcasted_iota(jnp.int32, (blk, blk), 1)
        dist = jnp.abs(row - col).astype(F32)
        visible = (col // CHUNK) <= (row // CHUNK)
        for h in range(RET_HEADS):
            decay_scr[h] = jnp.where(visible, jnp.exp(lg_ref[h] * dist), 0.0)

    rc = rc_ref[...]
    rs = rs_ref[...]
    half = RET_DK // 2
    pos = lax.broadcasted_iota(jnp.int32, (blk, 1), 0).astype(F32)
    for h in range(RET_HEADS):
        lg = lg_ref[h]
        ksl = slice(h * RET_DK, (h + 1) * RET_DK)
        vsl = slice(h * RET_DV, (h + 1) * RET_DV)
        qf = q_ref[:, ksl].astype(F32)
        kf = k_ref[:, ksl].astype(F32)
        q = qf * rc + pltpu.roll(qf, half, 1) * rs
        k = (kf * rc + pltpu.roll(kf, half, 1) * rs) * (RET_DK ** -0.5)
        v = v_ref[:, vsl]
        q_dec = jnp.exp(lg * (pos + 1.0))
        k_dec = jnp.exp(lg * (blk - 1.0 - pos))

        s = lax.dot_general(q.astype(BF16), k.astype(BF16), (((1,), (1,)), ((), ())), preferred_element_type=F32)
        s = s * decay_scr[h]
        state = state_scr[h]
        o = jnp.dot(s.astype(BF16), v, preferred_element_type=F32)
        o = o + jnp.dot((q * q_dec).astype(BF16), state.astype(BF16), preferred_element_type=F32)
        kd = (k * k_dec).astype(BF16)
        upd = lax.dot_general(kd, v, (((0,), (0,)), ((), ())), preferred_element_type=F32)
        state_scr[h] = state * jnp.exp(lg * blk) + upd

        o = o * lax.rsqrt(jnp.mean(o * o, axis=-1, keepdims=True) + EPS)
        g = g_ref[:, vsl].astype(F32)
        o_ref[:, vsl] = (g * _sigmoid(g) * o).astype(BF16)


def _retention(proj, rc, rs, batch, seq):
    t = proj.shape[0]
    blk = min(RET_BLOCK, seq)
    per_b = seq // blk
    log_gamma = jnp.log(1.0 - 2.0 ** (-5.0 - jnp.arange(RET_HEADS, dtype=F32)))
    qk_w = RET_HEADS * RET_DK
    v_w = RET_HEADS * RET_DV
    row = lambda b, n: b * per_b + n
    return pl.pallas_call(
        functools.partial(_ret_kernel, blk=blk),
        grid=(batch, per_b),
        in_specs=[
            pl.BlockSpec(memory_space=pltpu.SMEM),
            pl.BlockSpec((blk, qk_w), lambda b, n: (row(b, n), RQ_OFF // qk_w)),
            pl.BlockSpec((blk, qk_w), lambda b, n: (row(b, n), RK_OFF // qk_w)),
            pl.BlockSpec((blk, v_w), lambda b, n: (row(b, n), RV_OFF // v_w)),
            pl.BlockSpec((blk, v_w), lambda b, n: (row(b, n), RG_OFF // v_w)),
            pl.BlockSpec((blk, LANES), lambda b, n: (row(b, n), 0)),
            pl.BlockSpec((blk, LANES), lambda b, n: (row(b, n), 0)),
        ],
        out_specs=pl.BlockSpec((blk, v_w), lambda b, n: (row(b, n), 0)),
        out_shape=jax.ShapeDtypeStruct((t, v_w), BF16),
        scratch_shapes=[pltpu.VMEM((RET_HEADS, RET_DK, RET_DV), F32), pltpu.VMEM((RET_HEADS, blk, blk), F32)],
        compiler_params=_params(("arbitrary", "arbitrary")),
        name="retention",
    )(log_gamma, proj, proj, proj, proj, rc, rs)


def _mix_kernel(gate_ref, conv_ref, att_ref, ret_ref, x_ref, gt_ref, sh_ref, sc_ref, gf_ref, cw_ref,
                wom_ref, woc_ref, wor_ref, wmo_ref, wrh_ref, wrl_ref, br_ref,
                xo_ref, h_ref, rout_ref, cnt_ref, u_scr, *, tm, per_b):
    i = pl.program_id(0)
    d = D_MODEL
    cw = CONV_WIDTH

    cb = conv_ref[:, 0:cw].astype(F32)
    u = conv_ref[:, cw:2 * cw].astype(F32) * conv_ref[:, 2 * cw:3 * cw].astype(F32)

    @pl.when(i % per_b == 0)
    def _():
        u_scr[0:SUBLANES, :] = jnp.zeros((SUBLANES, cw), F32)

    @pl.when(i % per_b != 0)
    def _():
        u_scr[0:SUBLANES, :] = u_scr[tm:tm + SUBLANES, :]

    u_scr[SUBLANES:SUBLANES + tm, :] = u
    y = cw_ref[2:3, :] * u
    y = y + cw_ref[1:2, :] * u_scr[SUBLANES - 1:SUBLANES - 1 + tm, :]
    y = y + cw_ref[0:1, :] * u_scr[SUBLANES - 2:SUBLANES - 2 + tm, :]
    z = (cb * y).astype(BF16)

    y_mla = jnp.dot(att_ref[...], wom_ref[...], preferred_element_type=F32)
    y_conv = jnp.dot(z, woc_ref[...], preferred_element_type=F32)
    y_ret = jnp.dot(ret_ref[...], wor_ref[...], preferred_element_type=F32)
    merged = _sigmoid(gate_ref[:, 0:d].astype(F32)) * y_mla
    merged = merged + _sigmoid(gate_ref[:, d:2 * d].astype(F32)) * y_conv
    merged = merged + _sigmoid(gate_ref[:, 2 * d:3 * d].astype(F32)) * y_ret
    x = x_ref[...] + gt_ref[0] * jnp.dot(merged.astype(BF16), wmo_ref[...], preferred_element_type=F32)
    xo_ref[...] = x

    h = _rms(x, gf_ref[...]) * (1.0 + sc_ref[0]) + sh_ref[0]
    _store_token_tiles(h_ref, 0, tm, h)

    h_hi = h.astype(BF16)
    h_lo = (h - h_hi.astype(F32)).astype(BF16)
    lg = jnp.dot(h_hi, wrh_ref[...], preferred_element_type=F32)
    lg = lg + jnp.dot(h_lo, wrh_ref[...], preferred_element_type=F32)
    lg = lg + jnp.dot(h_hi, wrl_ref[...], preferred_element_type=F32)
    lg = lg + br_ref[...]

    lane = lax.broadcasted_iota(jnp.int32, lg.shape, 1)
    neg = -jnp.inf
    gl = jnp.where(lane < N_GROUPS, lg, neg)
    gmax = jnp.max(gl, axis=-1, keepdims=True)
    g_val = 1.0 / jnp.sum(jnp.exp(gl - gmax), axis=-1, keepdims=True)
    g_idx = jnp.min(jnp.where(gl == gmax, lane, LANES), axis=-1, keepdims=True)
    in_group = (lane >= ROUTE_E_OFF) & (lane < ROUTE_E_OFF + N_EXPERTS)
    in_group = in_group & (((lane - ROUTE_E_OFF) // EXPERTS_PER_GROUP) == g_idx)
    el = jnp.where(in_group, lg, neg)
    m1 = jnp.max(el, axis=-1, keepdims=True)
    i1 = jnp.min(jnp.where(el == m1, lane, LANES), axis=-1, keepdims=True)
    el2 = jnp.where(lane == i1, neg, el)
    m2 = jnp.max(el2, axis=-1, keepdims=True)
    i2 = jnp.min(jnp.where(el2 == m2, lane, LANES), axis=-1, keepdims=True)
    r = jnp.exp(m2 - m1)
    w1 = g_val / (1.0 + r)
    w2 = g_val * r / (1.0 + r)
    e1 = (i1 - ROUTE_E_OFF).astype(F32)
    e2 = (i2 - ROUTE_E_OFF).astype(F32)
    rout = jnp.where(lane == 0, e1, jnp.where(lane == 1, e2, jnp.where(lane == 2, w1, jnp.where(lane == 3, w2, 0.0))))
    rout_ref[...] = rout.T[0:SUBLANES, :]

    hits = jnp.where(lane == i1, 1.0, 0.0) + jnp.where(lane == i2, 1.0, 0.0)

    @pl.when(i == 0)
    def _():
        cnt_ref[...] = jnp.zeros(cnt_ref.shape, F32)

    cnt_ref[0:1, :] = cnt_ref[0:1, :] + jnp.sum(hits, axis=0, keepdims=True)


def _mix(proj, att, ret, x2, mod3, gf, cw, wom, woc, wor, wmo, wrh, wrl, br, seq):
    t, d = x2.shape
    tm = min(ROW_TILE, seq)
    per_b = seq // tm
    row = lambda i: (i, 0)
    modspec = lambda k: pl.BlockSpec((1, 1, d), lambda i: ((i // per_b) * 6 + k, 0, 0))
    return pl.pallas_call(
        functools.partial(_mix_kernel, tm=tm, per_b=per_b),
        grid=(t // tm,),
        in_specs=[
            pl.BlockSpec((tm, GATE_W), row),
            pl.BlockSpec((tm, 3 * CONV_WIDTH), lambda i: (i, CONV_OFF // (3 * CONV_WIDTH))),
            pl.BlockSpec((tm, MLA_HEADS * MLA_V), row),
            pl.BlockSpec((tm, RET_HEADS * RET_DV), row),
            pl.BlockSpec((tm, d), row),
            modspec(2), modspec(3), modspec(4),
            _const_spec((1, d)),
            _const_spec((SUBLANES, CONV_WIDTH)),
            _const_spec(wom.shape), _const_spec(woc.shape), _const_spec(wor.shape), _const_spec(wmo.shape),
            _const_spec(wrh.shape), _const_spec(wrl.shape), _const_spec((1, LANES)),
        ],
        out_specs=[pl.BlockSpec((tm, d), row), pl.BlockSpec((tm * ROW_SPLIT, LANES), row),
                   pl.BlockSpec((SUBLANES, tm), lambda i: (0, i)), _const_spec((SUBLANES, LANES))],
        out_shape=[
            jax.ShapeDtypeStruct((t, d), F32),
            jax.ShapeDtypeStruct((t * ROW_SPLIT, LANES), F32),
            jax.ShapeDtypeStruct((SUBLANES, t), F32),
            jax.ShapeDtypeStruct((SUBLANES, LANES), F32),
        ],
        scratch_shapes=[pltpu.VMEM((tm + SUBLANES, CONV_WIDTH), F32)],
        compiler_params=_params(("arbitrary",)),
        name="mix_router",
    )(proj, proj, att, ret, x2, mod3, mod3, mod3, gf, cw, wom, woc, wor, wmo, wrh, wrl, br)


def _moe_kernel(te_ref, nv_ref, meta_ref, meta_next_ref, wt_ref, wg_ref, wu_ref, wd_ref, h_hbm, out_hbm,
                xbuf, ybuf, wgb, wub, wdb, gsem, ssem, *, tm):
    i = pl.program_id(0)
    nt = pl.num_programs(0)
    slot = i % 2
    nv = nv_ref[i]
    prev = jnp.maximum(i - 1, 0)
    prev2 = jnp.maximum(i - 2, 0)

    def gather_copy(tok, s, r):
        src = h_hbm.at[pl.ds(pl.multiple_of(tok * ROW_SPLIT, ROW_SPLIT), ROW_SPLIT)]
        return pltpu.make_async_copy(src, xbuf.at[pl.ds((s * tm + r) * ROW_SPLIT, ROW_SPLIT)], gsem.at[s])

    def scatter_copy(dst, s, r):
        dst_rows = out_hbm.at[pl.ds(pl.multiple_of(dst * ROW_SPLIT, ROW_SPLIT), ROW_SPLIT)]
        return pltpu.make_async_copy(ybuf.at[pl.ds((s * tm + r) * ROW_SPLIT, ROW_SPLIT)], dst_rows, ssem.at[s])

    def start_gather(meta, s):
        for r in range(tm):
            gather_copy(meta[0, 0, r], s, r).start(priority=r % 2)

    def wait_gather(s):
        for r in range(tm):
            gather_copy(0, s, r).wait()

    def start_scatter(s):
        for r in range(tm):
            scatter_copy(meta_ref[0, 0, tm + r], s, r).start(priority=r % 2)

    def wait_scatter(s):
        for r in range(tm):
            scatter_copy(0, s, r).wait()

    def for_slot(fn):
        @pl.when(slot == 0)
        def _():
            fn(0)

        @pl.when(slot == 1)
        def _():
            fn(1)

    @pl.when(i == 0)
    def _():
        ybuf[...] = jnp.zeros(ybuf.shape, F32)
        half = tm * ROW_SPLIT
        for s in range(2):
            pad_rows = out_hbm.at[pl.ds(out_hbm.shape[0] - (2 - s) * half, half)]
            init = pltpu.make_async_copy(ybuf.at[pl.ds(s * half, half)], pad_rows, ssem.at[s])
            init.start()
            init.wait()
        start_gather(meta_ref, 0)

    @pl.when((i >= 2) & (nv_ref[prev2] > 0))
    def _():
        for_slot(wait_scatter)

    @pl.when((i == 0) | (te_ref[i] != te_ref[prev]))
    def _():
        wgb[...] = wg_ref[0].astype(BF16)
        wub[...] = wu_ref[0].astype(BF16)
        wdb[...] = wd_ref[0].astype(BF16)

    def active(s):
        wait_gather(s)
        x = _load_token_tiles(xbuf, s * tm * ROW_SPLIT, tm).astype(BF16)
        start_gather(meta_next_ref, 1 - s)
        g = jnp.dot(x, wgb[...], preferred_element_type=F32)
        u = jnp.dot(x, wub[...], preferred_element_type=F32)
        hid = (g * _sigmoid(g) * u).astype(BF16)
        w_col = jnp.broadcast_to(wt_ref[0], (LANES, tm)).T[:, 0:1]
        y = jnp.dot(hid, wdb[...], preferred_element_type=F32) * w_col
        _store_token_tiles(ybuf, s * tm * ROW_SPLIT, tm, y)
        start_scatter(s)

    @pl.when(nv > 0)
    def _():
        for_slot(active)

    @pl.when((i > 0) & (nv == 0) & (nv_ref[prev] > 0))
    def _():
        for_slot(wait_gather)

    @pl.when((i == nt - 1) & (nv_ref[prev] > 0))
    def _():
        for_slot(lambda s: wait_scatter(1 - s))


def _route_tables(rout, cnt, t, tm, nt):
    e = rout[0:2].astype(jnp.int32).reshape(-1)
    w = rout[2:4].reshape(-1)
    n_slot = 2 * t
    perm = jnp.argsort(e).astype(jnp.int32)
    counts = cnt[0, ROUTE_E_OFF:ROUTE_E_OFF + N_EXPERTS].astype(jnp.int32)
    nt_e = (counts + tm - 1) // tm
    tile_end = jnp.cumsum(nt_e)
    tile_start = tile_end - nt_e
    off = jnp.cumsum(counts) - counts
    total = tile_end[-1]
    i = jnp.arange(nt, dtype=jnp.int32)
    e_raw = jnp.minimum(jnp.sum((tile_end[None, :] <= i[:, None]).astype(jnp.int32), axis=1), N_EXPERTS - 1)
    e_i = jnp.where(i < total, e_raw, e_raw[jnp.maximum(total - 1, 0)])
    j = i - tile_start[e_i]
    nvalid = jnp.where(i < total, jnp.clip(counts[e_i] - j * tm, 0, tm), 0).astype(jnp.int32)
    r = jnp.arange(tm, dtype=jnp.int32)
    pos = jnp.clip((off[e_i] + j * tm)[:, None] + r[None, :], 0, n_slot - 1)
    slot = perm[pos]
    valid = r[None, :] < nvalid[:, None]
    tok = jnp.where(slot >= t, slot - t, slot)
    dst = jnp.where(valid, slot, n_slot + (i % 2)[:, None] * tm + r[None, :])
    wt = jnp.where(valid, w[slot], 0.0)
    meta = jnp.concatenate([tok, dst], axis=1).astype(jnp.int32).reshape(nt, 1, 2 * tm)
    return e_i, nvalid, meta, wt.reshape(nt, 1, tm)


def _moe(h, rout, cnt, wg, wu, wd, layer):
    t, d = h.shape[0] // ROW_SPLIT, D_MODEL
    tm = min(MOE_TILE, t)
    nt = (2 * t) // tm + N_EXPERTS
    e_i, nvalid, meta, wt = _route_tables(rout, cnt, t, tm, nt)
    f = wg.shape[-1]
    grid_spec = pltpu.PrefetchScalarGridSpec(
        num_scalar_prefetch=2,
        grid=(nt,),
        in_specs=[
            pl.BlockSpec((1, 1, 2 * tm), lambda i, te, nv: (i, 0, 0), memory_space=pltpu.SMEM),
            pl.BlockSpec((1, 1, 2 * tm), lambda i, te, nv: (jnp.minimum(i + 1, nt - 1), 0, 0), memory_space=pltpu.SMEM),
            pl.BlockSpec((1, 1, tm), lambda i, te, nv: (i, 0, 0)),
            pl.BlockSpec((1, 1, d, f), lambda i, te, nv: (layer, te[i], 0, 0)),
            pl.BlockSpec((1, 1, d, f), lambda i, te, nv: (layer, te[i], 0, 0)),
            pl.BlockSpec((1, 1, f, d), lambda i, te, nv: (layer, te[i], 0, 0)),
            pl.BlockSpec(memory_space=pl.ANY),
        ],
        out_specs=pl.BlockSpec(memory_space=pl.ANY),
        scratch_shapes=[
            pltpu.VMEM((2 * tm * ROW_SPLIT, LANES), F32),
            pltpu.VMEM((2 * tm * ROW_SPLIT, LANES), F32),
            pltpu.VMEM((d, f), BF16),
            pltpu.VMEM((d, f), BF16),
            pltpu.VMEM((f, d), BF16),
            pltpu.SemaphoreType.DMA((2,)),
            pltpu.SemaphoreType.DMA((2,)),
        ],
    )

    def kern(te_ref, nv_ref, meta_ref, meta_next_ref, wt_ref, wg_ref, wu_ref, wd_ref, h_hbm, out_hbm, *scratch):
        _moe_kernel(te_ref, nv_ref, meta_ref, meta_next_ref, wt_ref, wg_ref.at[0], wu_ref.at[0], wd_ref.at[0],
                    h_hbm, out_hbm, *scratch, tm=tm)

    return pl.pallas_call(
        kern,
        grid_spec=grid_spec,
        out_shape=jax.ShapeDtypeStruct(((2 * t + 2 * tm) * ROW_SPLIT, LANES), F32),
        compiler_params=_params(("arbitrary",)),
        name="moe_experts",
    )(e_i, nvalid, meta, meta, wt, wg, wu, wd, h)


def _combine_kernel(x_ref, a_ref, b_ref, gt_ref, g_ref, o_ref, *, final):
    tm = x_ref.shape[0]
    moe = _load_token_tiles(a_ref, 0, tm) + _load_token_tiles(b_ref, 0, tm)
    x = x_ref[...] + gt_ref[0] * moe
    if final:
        x = _rms(x, g_ref[...])
    o_ref[...] = x


def _combine(x2, moe_out, mod3, final_g, seq, final):
    t, d = x2.shape
    tm = min(512, seq)
    per_b = seq // tm
    return pl.pallas_call(
        functools.partial(_combine_kernel, final=final),
        grid=(t // tm,),
        in_specs=[
            pl.BlockSpec((tm, d), lambda i: (i, 0)),
            pl.BlockSpec((tm * ROW_SPLIT, LANES), lambda i: (i, 0)),
            pl.BlockSpec((tm * ROW_SPLIT, LANES), lambda i: (t // tm + i, 0)),
            pl.BlockSpec((1, 1, d), lambda i: ((i // per_b) * 6 + 5, 0, 0)),
            _const_spec((1, d)),
        ],
        out_specs=pl.BlockSpec((tm, d), lambda i: (i, 0)),
        out_shape=jax.ShapeDtypeStruct((t, d), F32),
        compiler_params=_params(("arbitrary",)),
        name="combine",
    )(x2, moe_out, moe_out, mod3, final_g)


def _prep_w_in(w):
    d = w.shape[0]
    sizes = (MLA_Q_RANK, MLA_KV_RANK, MLA_ROPE, CONV_WIDTH, CONV_WIDTH, CONV_WIDTH,
             RET_HEADS * RET_DK, RET_HEADS * RET_DK, RET_HEADS * RET_DV, RET_HEADS * RET_DV, 3 * D_MODEL)
    parts = []
    start = 0
    for size in sizes:
        parts.append(w[:, start:start + size])
        start += size
    q_lat, kv_lat, kr, cb, cc, cx, rq, rk, rv, rg, gl = parts
    half = MLA_ROPE // 2
    zeros = lambda n: jnp.zeros((d, n), w.dtype)
    kblock = jnp.concatenate(
        [-kr[:, half:], kr[:, :half], zeros(MLA_NOPE - MLA_ROPE), kr, zeros(LANES - MLA_NOPE - MLA_ROPE)], axis=1)
    out = jnp.concatenate([gl, rq, rk, rv, rg, cb, cc, cx, q_lat, kv_lat, kblock], axis=1)
    return out.astype(BF16)


def _prep_w_uq(w):
    r = w.shape[0]
    dq = MLA_NOPE + MLA_ROPE
    half = MLA_ROPE // 2
    w3 = w.reshape(r, MLA_HEADS, dq)
    pad = QPAD - dq
    wa = jnp.pad(w3, ((0, 0), (0, 0), (0, pad))).reshape(r, MLA_HEADS * QPAD)
    wb = jnp.concatenate(
        [jnp.zeros((r, MLA_HEADS, MLA_NOPE), w.dtype), -w3[:, :, MLA_NOPE + half:], w3[:, :, MLA_NOPE:MLA_NOPE + half],
         jnp.zeros((r, MLA_HEADS, pad), w.dtype)], axis=2).reshape(r, MLA_HEADS * QPAD)
    return wa.astype(BF16), wb.astype(BF16)


def _prep_w_ukv(w):
    r = w.shape[0]
    w3 = w.reshape(r, MLA_HEADS, MLA_NOPE + MLA_V)
    wk = jnp.pad(w3[:, :, :MLA_NOPE], ((0, 0), (0, 0), (0, QPAD - MLA_NOPE))).reshape(r, MLA_HEADS * QPAD)
    wv = jnp.pad(w3[:, :, MLA_NOPE:], ((0, 0), (0, 0), (0, QPAD - MLA_V))).reshape(r, MLA_HEADS * QPAD)
    return wk.astype(BF16), wv.T.astype(BF16)


def _prep_router(w_rg, b_rg, w_re, b_re):
    d = w_rg.shape[0]
    w = jnp.zeros((d, LANES), F32)
    w = w.at[:, :N_GROUPS].set(w_rg).at[:, ROUTE_E_OFF:ROUTE_E_OFF + N_EXPERTS].set(w_re)
    b = jnp.zeros((1, LANES), F32)
    b = b.at[0, :N_GROUPS].set(b_rg).at[0, ROUTE_E_OFF:ROUTE_E_OFF + N_EXPERTS].set(b_re)
    w_hi = w.astype(BF16)
    w_lo = (w - w_hi.astype(F32)).astype(BF16)
    return w_hi, w_lo, b


def kernel(x, c, positions, w_ada, b_ada, norm_mix_g, norm_ffn_g, w_in, mla_q_norm_g, mla_kv_norm_g, w_uq, w_ukv, w_o_mla, conv_w, w_o_conv, w_o_ret, w_mix_out, w_route_group, b_route_group, w_route_expert, b_route_expert, w_exp_gate, w_exp_up, w_exp_down, final_g):
    batch, seq, d = x.shape
    depth = w_in.shape[0]
    t = batch * seq
    x2 = x.reshape(t, d)
    mod = _ada_mod(c, w_ada, b_ada)
    rc, rs, mc, ms = _rope_tables(positions)
    for l in range(depth):
        mod3 = mod[l].reshape(batch * 6, 1, d)
        proj = _in_proj(x2, mod3, norm_mix_g[l].reshape(1, d), _prep_w_in(w_in[l]), seq)
        wa, wb = _prep_w_uq(w_uq[l])
        wk, wv = _prep_w_ukv(w_ukv[l])
        q, k, v = _mla_prep(proj, mc, ms, mla_q_norm_g[l].reshape(1, -1), mla_kv_norm_g[l].reshape(1, -1), wa, wb, wk, wv,
                            batch, seq)
        att = _attention(q, k, v, batch, seq)
        ret = _retention(proj, rc, rs, batch, seq)
        wrh, wrl, br = _prep_router(w_route_group[l], b_route_group[l], w_route_expert[l], b_route_expert[l])
        cw = jnp.pad(conv_w[l], ((0, SUBLANES - CONV_K), (0, 0)))
        x2, h, rout, cnt = _mix(proj, att, ret, x2, mod3, norm_ffn_g[l].reshape(1, d), cw,
                           w_o_mla[l].astype(BF16), w_o_conv[l].astype(BF16), w_o_ret[l].astype(BF16),
                           w_mix_out[l].astype(BF16), wrh, wrl, br, seq)
        moe_out = _moe(h, rout, cnt, w_exp_gate, w_exp_up, w_exp_down, l)
        x2 = _combine(x2, moe_out, mod3, final_g.reshape(1, d), seq, final=(l == depth - 1))
    return x2.reshape(batch, seq, d)
```

```python
import functools

import jax
import jax.numpy as jnp
from jax import lax
from jax.experimental import pallas as pl
from jax.experimental.pallas import tpu as pltpu

F32 = jnp.float32
BF16 = jnp.bfloat16

D_MODEL = 1024
CHUNK = 64
EPS = 1e-6
ROPE_THETA = 10000.0
LOG2E = 1.4426950408889634

MLA_HEADS = 8
MLA_Q_RANK = 384
MLA_KV_RANK = 256
MLA_NOPE = 64
MLA_ROPE = 32
MLA_V = 64

CONV_WIDTH = 512
CONV_K = 3

RET_HEADS = 4
RET_DK = 128
RET_DV = 256

N_GROUPS = 4
EXPERTS_PER_GROUP = 8
N_EXPERTS = N_GROUPS * EXPERTS_PER_GROUP
EXPERT_FF = 512

LANES = 128
SUBLANES = 8
VMEM_LIMIT = 52 * 1024 * 1024

GATE_W = 3 * D_MODEL
RQ_OFF = GATE_W
RK_OFF = RQ_OFF + RET_HEADS * RET_DK
RV_OFF = RK_OFF + RET_HEADS * RET_DK
RG_OFF = RV_OFF + RET_HEADS * RET_DV
CONV_OFF = RG_OFF + RET_HEADS * RET_DV
MLA_OFF = CONV_OFF + 3 * CONV_WIDTH
MLA_W = MLA_Q_RANK + MLA_KV_RANK + LANES
PROJ_W = MLA_OFF + MLA_W
PROJ_N_TILES = 3
PROJ_TN = PROJ_W // PROJ_N_TILES
assert PROJ_TN * PROJ_N_TILES == PROJ_W and PROJ_TN % LANES == 0
assert MLA_OFF % MLA_W == 0 and CONV_OFF % (3 * CONV_WIDTH) == 0

QPAD = LANES
ROUTE_E_OFF = 32

ROW_TILE_PROJ = 1024
ROW_TILE = 512
ATT_TILE = 512
ATT_SUB = 256
ATT_GROUP = 4
LAZY_MAX_HEADROOM = 60.0
RET_BLOCK = 256
MOE_TILE = 256
ADA_TN = 768


def _params(sem, vmem=VMEM_LIMIT):
    return pltpu.CompilerParams(dimension_semantics=sem, vmem_limit_bytes=vmem)


def _const_spec(shape):
    nd = len(shape)
    return pl.BlockSpec(shape, lambda *_: (0,) * nd)


ROW_SPLIT = D_MODEL // LANES
assert ROW_SPLIT == SUBLANES


def _store_token_tiles(ref, base, rows, val):
    for s in range(ROW_SPLIT):
        ref[pl.ds(base + s, rows, stride=ROW_SPLIT), :] = val[:, s * LANES:(s + 1) * LANES]


def _load_token_tiles(ref, base, rows):
    return jnp.concatenate([ref[pl.ds(base + s, rows, stride=ROW_SPLIT), :] for s in range(ROW_SPLIT)], axis=1)


def _rms(x, g):
    return x * lax.rsqrt(jnp.mean(x * x, axis=-1, keepdims=True) + EPS) * g


def _sigmoid(x):
    return 1.0 / (1.0 + jnp.exp(-x))


def _ada_kernel(ct_ref, w_ref, b_ref, o_ref):
    ct = ct_ref[...]
    a = ct * _sigmoid(ct)
    w = w_ref[0]
    rows = [jnp.sum(w * a[:, b:b + 1], axis=0, keepdims=True) for b in range(ct.shape[1])]
    o_ref[0] = jnp.concatenate(rows, axis=0) + b_ref[0]


def _ada_mod(c, w_ada, b_ada):
    depth, d, n = w_ada.shape
    b = c.shape[0]
    return pl.pallas_call(
        _ada_kernel,
        grid=(depth, n // ADA_TN),
        in_specs=[
            _const_spec((d, b)),
            pl.BlockSpec((1, d, ADA_TN), lambda l, j: (l, 0, j)),
            pl.BlockSpec((1, 1, ADA_TN), lambda l, j: (l, 0, j)),
        ],
        out_specs=pl.BlockSpec((1, b, ADA_TN), lambda l, j: (l, 0, j)),
        out_shape=jax.ShapeDtypeStruct((depth, b, n), F32),
        compiler_params=_params(("arbitrary", "arbitrary")),
        name="ada_mod",
    )(c.T, w_ada, b_ada.reshape(depth, 1, n))


def _rope_kernel(pos_ref, f_ref, rc_ref, rs_ref, mc_ref, ms_ref):
    ang = pos_ref[...].astype(F32) * f_ref[...]
    cs = jnp.cos(ang)
    sn = jnp.sin(ang)
    lane = lax.broadcasted_iota(jnp.int32, ang.shape, 1)
    half = LANES // 2
    lo = lane < half
    rc_ref[...] = jnp.where(lo, cs, pltpu.roll(cs, half, 1))
    rs_ref[...] = jnp.where(lo, -sn, pltpu.roll(sn, half, 1))
    rope = (lane >= MLA_NOPE) & (lane < MLA_NOPE + MLA_ROPE)
    mc_ref[...] = jnp.where(lane < MLA_NOPE, 1.0, jnp.where(rope, cs, 0.0))
    ms_ref[...] = jnp.where(rope, sn, 0.0)


def _rope_tables(positions):
    t = positions.size
    tm = min(2048, t)
    inv_ret = ROPE_THETA ** (-jnp.arange(0, RET_DK, 2, dtype=F32) / RET_DK)
    inv_mla = ROPE_THETA ** (-jnp.arange(0, MLA_ROPE, 2, dtype=F32) / MLA_ROPE)
    freqs = jnp.concatenate([inv_ret, inv_mla, inv_mla, jnp.zeros((LANES - 96,), F32)]).reshape(1, LANES)
    spec = pl.BlockSpec((tm, LANES), lambda i: (i, 0))
    return pl.pallas_call(
        _rope_kernel,
        grid=(t // tm,),
        in_specs=[pl.BlockSpec((tm, 1), lambda i: (i, 0)), _const_spec((1, LANES))],
        out_specs=[spec] * 4,
        out_shape=[jax.ShapeDtypeStruct((t, LANES), F32)] * 4,
        compiler_params=_params(("arbitrary",)),
        name="rope_tables",
    )(positions.reshape(t, 1), freqs)


def _proj_kernel(x_ref, sh_ref, sc_ref, g_ref, w_ref, o_ref, h_scr):
    @pl.when(pl.program_id(1) == 0)
    def _():
        h = _rms(x_ref[...], g_ref[...]) * (1.0 + sc_ref[0]) + sh_ref[0]
        h_scr[...] = h.astype(BF16)

    o_ref[...] = jnp.dot(h_scr[...], w_ref[...], preferred_element_type=F32).astype(BF16)


def _in_proj(x2, mod3, g, w, seq):
    t, d = x2.shape
    tm = min(ROW_TILE_PROJ, seq)
    per_b = seq // tm
    return pl.pallas_call(
        _proj_kernel,
        grid=(t // tm, PROJ_W // PROJ_TN),
        in_specs=[
            pl.BlockSpec((tm, d), lambda i, j: (i, 0)),
            pl.BlockSpec((1, 1, d), lambda i, j: ((i // per_b) * 6 + 0, 0, 0)),
            pl.BlockSpec((1, 1, d), lambda i, j: ((i // per_b) * 6 + 1, 0, 0)),
            _const_spec((1, d)),
            pl.BlockSpec((d, PROJ_TN), lambda i, j: (0, j)),
        ],
        out_specs=pl.BlockSpec((tm, PROJ_TN), lambda i, j: (i, j)),
        out_shape=jax.ShapeDtypeStruct((t, PROJ_W), BF16),
        scratch_shapes=[pltpu.VMEM((tm, d), BF16)],
        compiler_params=_params(("arbitrary", "arbitrary")),
        name="in_proj",
    )(x2, mod3, mod3, g, w)


def _mla_prep_kernel(in_ref, mc_ref, ms_ref, gq_ref, gkv_ref, wa_ref, wb_ref, wk_ref, wv_ref, q_ref, k_ref, v_ref):
    blk = in_ref[...].astype(F32)
    qn = _rms(blk[:, :MLA_Q_RANK], gq_ref[...]).astype(BF16)
    kvn = _rms(blk[:, MLA_Q_RANK:MLA_Q_RANK + MLA_KV_RANK], gkv_ref[...]).astype(BF16)
    kb = blk[:, MLA_Q_RANK + MLA_KV_RANK:]
    mc = mc_ref[...]
    ms = ms_ref[...]
    qa = jnp.dot(qn, wa_ref[...], preferred_element_type=F32)
    qb = jnp.dot(qn, wb_ref[...], preferred_element_type=F32)
    kn = jnp.dot(kvn, wk_ref[...], preferred_element_type=F32)
    lane = lax.broadcasted_iota(jnp.int32, kb.shape, 1)
    rope = (lane >= MLA_NOPE) & (lane < MLA_NOPE + MLA_ROPE)
    kpe = jnp.where(rope, kb * mc + pltpu.roll(kb, LANES // 2, 1) * ms, 0.0)
    scale = (MLA_NOPE + MLA_ROPE) ** -0.5 * LOG2E
    for h in range(MLA_HEADS):
        sl = slice(h * QPAD, (h + 1) * QPAD)
        q_ref[:, sl] = ((qa[:, sl] * mc + qb[:, sl] * ms) * scale).astype(BF16)
        k_ref[:, sl] = (kn[:, sl] + kpe).astype(BF16)
    vt = lax.dot_general(wv_ref[...], kvn, (((1,), (1,)), ((), ())), preferred_element_type=F32)
    row = lax.broadcasted_iota(jnp.int32, vt.shape, 0)
    v_ref[0, 0] = jnp.where(row % QPAD >= MLA_V, 1.0, vt).astype(BF16)


def _mla_prep(proj, mc, ms, gq, gkv, wa, wb, wk, wv, batch, seq):
    t = proj.shape[0]
    tm = min(ATT_TILE, seq)
    per_b = seq // tm
    hq = MLA_HEADS * QPAD
    hv = MLA_HEADS * QPAD
    return pl.pallas_call(
        _mla_prep_kernel,
        grid=(t // tm,),
        in_specs=[
            pl.BlockSpec((tm, MLA_W), lambda i: (i, MLA_OFF // MLA_W)),
            pl.BlockSpec((tm, LANES), lambda i: (i, 0)),
            pl.BlockSpec((tm, LANES), lambda i: (i, 0)),
            _const_spec((1, MLA_Q_RANK)),
            _const_spec((1, MLA_KV_RANK)),
            _const_spec((MLA_Q_RANK, hq)),
            _const_spec((MLA_Q_RANK, hq)),
            _const_spec((MLA_KV_RANK, hq)),
            _const_spec((hv, MLA_KV_RANK)),
        ],
        out_specs=[
            pl.BlockSpec((tm, hq), lambda i: (i, 0)),
            pl.BlockSpec((tm, hq), lambda i: (i, 0)),
            pl.BlockSpec((1, 1, hv, tm), lambda i: (i // per_b, i % per_b, 0, 0)),
        ],
        out_shape=[
            jax.ShapeDtypeStruct((t, hq), BF16),
            jax.ShapeDtypeStruct((t, hq), BF16),
            jax.ShapeDtypeStruct((batch, per_b, hv, tm), BF16),
        ],
        compiler_params=_params(("arbitrary",)),
        name="mla_prep",
    )(proj, mc, ms, gq, gkv, wa, wb, wk, wv)


def _attn_kernel(q_ref, k_ref, vt_ref, o_ref, m_scr, acc_scr, *, tile):
    qi = pl.program_id(2)
    nn_dims = (((1,), (1,)), ((), ()))

    def head_q(a):
        return q_ref[0, :, a * QPAD:(a + 1) * QPAD]

    def head_k(start, size, a):
        return k_ref[0, pl.ds(start, size), a * QPAD:(a + 1) * QPAD]

    def exact_step(j, src, dst, first, masked):
        start = pl.multiple_of(j * tile, tile)
        for a in range(2):
            st = lax.dot_general(head_k(start, tile, a), head_q(a), nn_dims, preferred_element_type=F32)
            if masked:
                key = lax.broadcasted_iota(jnp.int32, (tile, tile), 0)
                qry = lax.broadcasted_iota(jnp.int32, (tile, tile), 1)
                st = jnp.where((key // CHUNK) <= (qry // CHUNK), st, -jnp.inf)
            m_new = jnp.max(st, axis=0, keepdims=True)
            if not first:
                m_old = m_scr[src, a]
                m_new = jnp.maximum(m_old, m_new)
            p = jnp.exp2(st - m_new).astype(BF16)
            vt = vt_ref[0, j, a * QPAD:(a + 1) * QPAD, :]
            pv = jnp.dot(vt, p, preferred_element_type=F32)
            if not first:
                pv = pv + jnp.exp2(m_old - m_new) * acc_scr[src, a]
            acc_scr[dst, a] = pv
            m_scr[dst, a] = m_new

    exact_step(qi, 0, 0, first=True, masked=True)

    def lazy_step(j0, n_tiles, cur):
        nxt = 1 - cur
        n_sub = tile // ATT_SUB
        scores = {}
        for a in range(2):
            for t in range(n_tiles):
                start = pl.multiple_of((j0 + t) * tile, tile)
                for s in range(n_sub):
                    k = head_k(start + s * ATT_SUB, ATT_SUB, a)
                    scores[a, t, s] = lax.dot_general(k, head_q(a), nn_dims, preferred_element_type=F32)
        excess = jnp.zeros((1, tile), F32)
        for a in range(2):
            m_used = m_scr[cur, a]
            m_new = m_used
            pv = jnp.zeros((QPAD, tile), F32)
            for t in range(n_tiles):
                for s in range(n_sub):
                    st = scores[a, t, s]
                    m_new = jnp.maximum(m_new, jnp.max(st, axis=0, keepdims=True))
                    p = jnp.exp2(st - m_used).astype(BF16)
                    vt = vt_ref[0, j0 + t, a * QPAD:(a + 1) * QPAD, s * ATT_SUB:(s + 1) * ATT_SUB]
                    pv = pv + jnp.dot(vt, p, preferred_element_type=F32)
            acc_scr[nxt, a] = (acc_scr[cur, a] + pv) * jnp.exp2(m_used - m_new)
            m_scr[nxt, a] = m_new
            excess = jnp.maximum(excess, m_new - m_used)

        @pl.when(jnp.max(excess) > LAZY_MAX_HEADROOM)
        def _():
            def redo(t, carry):
                exact_step(j0 + t, jnp.where(t == 0, cur, nxt), nxt, first=False, masked=False)
                return carry

            lax.fori_loop(0, n_tiles, redo, 0)

        return nxt

    n_groups = qi // ATT_GROUP
    cur = lax.fori_loop(0, n_groups, lambda g, c: lazy_step(g * ATT_GROUP, ATT_GROUP, c), 0)
    cur = lax.fori_loop(n_groups * ATT_GROUP, qi, lambda j, c: lazy_step(j, 1, c), cur)

    lane = lax.broadcasted_iota(jnp.int32, (tile, LANES), 1)
    r0 = acc_scr[cur, 0].T
    r1 = acc_scr[cur, 1].T
    o0 = r0 / r0[:, MLA_V:MLA_V + 1]
    o1 = pltpu.roll(r1, MLA_V, 1) / r1[:, MLA_V:MLA_V + 1]
    o_ref[0] = jnp.where(lane < MLA_V, o0, o1).astype(BF16)


def _attention(q, k, vt, batch, seq):
    tile = min(ATT_TILE, seq)
    q3 = q.reshape(batch, seq, MLA_HEADS * QPAD)
    k3 = k.reshape(batch, seq, MLA_HEADS * QPAD)
    out = pl.pallas_call(
        functools.partial(_attn_kernel, tile=tile),
        grid=(batch, MLA_HEADS // 2, seq // tile),
        in_specs=[
            pl.BlockSpec((1, tile, 2 * QPAD), lambda b, h, i: (b, i, h)),
            pl.BlockSpec((1, seq, 2 * QPAD), lambda b, h, i: (b, 0, h)),
            pl.BlockSpec((1, seq // tile, 2 * QPAD, tile), lambda b, h, i: (b, 0, h, 0)),
        ],
        out_specs=pl.BlockSpec((1, tile, 2 * MLA_V), lambda b, h, i: (b, i, h)),
        out_shape=jax.ShapeDtypeStruct((batch, seq, MLA_HEADS * MLA_V), BF16),
        scratch_shapes=[
            pltpu.VMEM((2, 2, 1, tile), F32),
            pltpu.VMEM((2, 2, QPAD, tile), F32),
        ],
        compiler_params=_params(("arbitrary", "arbitrary", "arbitrary")),
        name="mla_attention",
    )(q3, k3, vt)
    return out.reshape(batch * seq, MLA_HEADS * MLA_V)


def _ret_kernel(lg_ref, q_ref, k_ref, v_ref, g_ref, rc_ref, rs_ref, o_ref, state_scr, decay_scr, *, blk):
    n = pl.program_id(1)

    @pl.when(n == 0)
    def _():
        state_scr[...] = jnp.zeros(state_scr.shape, F32)
        row = lax.broadcasted_iota(jnp.int32, (blk, blk), 0)
        col = lax.broadcasted_iota(jnp.int32, (blk, blk), 1)
        dist = jnp.abs(row - col).astype(F32)
        visible = (col // CHUNK) <= (row // CHUNK)
        for h in range(RET_HEADS):
            decay_scr[h] = jnp.where(visible, jnp.exp(lg_ref[h] * dist), 0.0)

    rc = rc_ref[...]
    rs = rs_ref[...]
    half = RET_DK // 2
    pos = lax.broadcasted_iota(jnp.int32, (blk, 1), 0).astype(F32)
    for h in range(RET_HEADS):
        lg = lg_ref[h]
        ksl = slice(h * RET_DK, (h + 1) * RET_DK)
        vsl = slice(h * RET_DV, (h + 1) * RET_DV)
        qf = q_ref[:, ksl].astype(F32)
        kf = k_ref[:, ksl].astype(F32)
        q = qf * rc + pltpu.roll(qf, half, 1) * rs
        k = (kf * rc + pltpu.roll(kf, half, 1) * rs) * (RET_DK ** -0.5)
        v = v_ref[:, vsl]
        q_dec = jnp.exp(lg * (pos + 1.0))
        k_dec = jnp.exp(lg * (blk - 1.0 - pos))

        s = lax.dot_general(q.astype(BF16), k.astype(BF16), (((1,), (1,)), ((), ())), preferred_element_type=F32)
        s = s * decay_scr[h]
        state = state_scr[h]
        o = jnp.dot(s.astype(BF16), v, preferred_element_type=F32)
        o = o + jnp.dot((q * q_dec).astype(BF16), state.astype(BF16), preferred_element_type=F32)
        kd = (k * k_dec).astype(BF16)
        upd = lax.dot_general(kd, v, (((0,), (0,)), ((), ())), preferred_element_type=F32)
        state_scr[h] = state * jnp.exp(lg * blk) + upd

        o = o * lax.rsqrt(jnp.mean(o * o, axis=-1, keepdims=True) + EPS)
        g = g_ref[:, vsl].astype(F32)
        o_ref[:, vsl] = (g * _sigmoid(g) * o).astype(BF16)


def _retention(proj, rc, rs, batch, seq):
    t = proj.shape[0]
    blk = min(RET_BLOCK, seq)
    per_b = seq // blk
    log_gamma = jnp.log(1.0 - 2.0 ** (-5.0 - jnp.arange(RET_HEADS, dtype=F32)))
    qk_w = RET_HEADS * RET_DK
    v_w = RET_HEADS * RET_DV
    row = lambda b, n: b * per_b + n
    return pl.pallas_call(
        functools.partial(_ret_kernel, blk=blk),
        grid=(batch, per_b),
        in_specs=[
            pl.BlockSpec(memory_space=pltpu.SMEM),
            pl.BlockSpec((blk, qk_w), lambda b, n: (row(b, n), RQ_OFF // qk_w)),
            pl.BlockSpec((blk, qk_w), lambda b, n: (row(b, n), RK_OFF // qk_w)),
            pl.BlockSpec((blk, v_w), lambda b, n: (row(b, n), RV_OFF // v_w)),
            pl.BlockSpec((blk, v_w), lambda b, n: (row(b, n), RG_OFF // v_w)),
            pl.BlockSpec((blk, LANES), lambda b, n: (row(b, n), 0)),
            pl.BlockSpec((blk, LANES), lambda b, n: (row(b, n), 0)),
        ],
        out_specs=pl.BlockSpec((blk, v_w), lambda b, n: (row(b, n), 0)),
        out_shape=jax.ShapeDtypeStruct((t, v_w), BF16),
        scratch_shapes=[pltpu.VMEM((RET_HEADS, RET_DK, RET_DV), F32), pltpu.VMEM((RET_HEADS, blk, blk), F32)],
        compiler_params=_params(("arbitrary", "arbitrary")),
        name="retention",
    )(log_gamma, proj, proj, proj, proj, rc, rs)


def _mix_kernel(gate_ref, conv_ref, att_ref, ret_ref, x_ref, gt_ref, sh_ref, sc_ref, gf_ref, cw_ref,
                wom_ref, woc_ref, wor_ref, wmo_ref, wrh_ref, wrl_ref, br_ref,
                xo_ref, h_ref, rout_ref, cnt_ref, u_scr, *, tm, per_b):
    i = pl.program_id(0)
    d = D_MODEL
    cw = CONV_WIDTH

    cb = conv_ref[:, 0:cw].astype(F32)
    u = conv_ref[:, cw:2 * cw].astype(F32) * conv_ref[:, 2 * cw:3 * cw].astype(F32)

    @pl.when(i % per_b == 0)
    def _():
        u_scr[0:SUBLANES, :] = jnp.zeros((SUBLANES, cw), F32)

    @pl.when(i % per_b != 0)
    def _():
        u_scr[0:SUBLANES, :] = u_scr[tm:tm + SUBLANES, :]

    u_scr[SUBLANES:SUBLANES + tm, :] = u
    y = cw_ref[2:3, :] * u
    y = y + cw_ref[1:2, :] * u_scr[SUBLANES - 1:SUBLANES - 1 + tm, :]
    y = y + cw_ref[0:1, :] * u_scr[SUBLANES - 2:SUBLANES - 2 + tm, :]
    z = (cb * y).astype(BF16)

    y_mla = jnp.dot(att_ref[...], wom_ref[...], preferred_element_type=F32)
    y_conv = jnp.dot(z, woc_ref[...], preferred_element_type=F32)
    y_ret = jnp.dot(ret_ref[...], wor_ref[...], preferred_element_type=F32)
    merged = _sigmoid(gate_ref[:, 0:d].astype(F32)) * y_mla
    merged = merged + _sigmoid(gate_ref[:, d:2 * d].astype(F32)) * y_conv
    merged = merged + _sigmoid(gate_ref[:, 2 * d:3 * d].astype(F32)) * y_ret
    x = x_ref[...] + gt_ref[0] * jnp.dot(merged.astype(BF16), wmo_ref[...], preferred_element_type=F32)
    xo_ref[...] = x

    h = _rms(x, gf_ref[...]) * (1.0 + sc_ref[0]) + sh_ref[0]
    _store_token_tiles(h_ref, 0, tm, h)

    h_hi = h.astype(BF16)
    h_lo = (h - h_hi.astype(F32)).astype(BF16)
    lg = jnp.dot(h_hi, wrh_ref[...], preferred_element_type=F32)
    lg = lg + jnp.dot(h_lo, wrh_ref[...], preferred_element_type=F32)
    lg = lg + jnp.dot(h_hi, wrl_ref[...], preferred_element_type=F32)
    lg = lg + br_ref[...]

    lane = lax.broadcasted_iota(jnp.int32, lg.shape, 1)
    neg = -jnp.inf
    gl = jnp.where(lane < N_GROUPS, lg, neg)
    gmax = jnp.max(gl, axis=-1, keepdims=True)
    g_val = 1.0 / jnp.sum(jnp.exp(gl - gmax), axis=-1, keepdims=True)
    g_idx = jnp.min(jnp.where(gl == gmax, lane, LANES), axis=-1, keepdims=True)
    in_group = (lane >= ROUTE_E_OFF) & (lane < ROUTE_E_OFF + N_EXPERTS)
    in_group = in_group & (((lane - ROUTE_E_OFF) // EXPERTS_PER_GROUP) == g_idx)
    el = jnp.where(in_group, lg, neg)
    m1 = jnp.max(el, axis=-1, keepdims=True)
    i1 = jnp.min(jnp.where(el == m1, lane, LANES), axis=-1, keepdims=True)
    el2 = jnp.where(lane == i1, neg, el)
    m2 = jnp.max(el2, axis=-1, keepdims=True)
    i2 = jnp.min(jnp.where(el2 == m2, lane, LANES), axis=-1, keepdims=True)
    r = jnp.exp(m2 - m1)
    w1 = g_val / (1.0 + r)
    w2 = g_val * r / (1.0 + r)
    e1 = (i1 - ROUTE_E_OFF).astype(F32)
    e2 = (i2 - ROUTE_E_OFF).astype(F32)
    rout = jnp.where(lane == 0, e1, jnp.where(lane == 1, e2, jnp.where(lane == 2, w1, jnp.where(lane == 3, w2, 0.0))))
    rout_ref[...] = rout.T[0:SUBLANES, :]

    hits = jnp.where(lane == i1, 1.0, 0.0) + jnp.where(lane == i2, 1.0, 0.0)

    @pl.when(i == 0)
    def _():
        cnt_ref[...] = jnp.zeros(cnt_ref.shape, F32)

    cnt_ref[0:1, :] = cnt_ref[0:1, :] + jnp.sum(hits, axis=0, keepdims=True)


def _mix(proj, att, ret, x2, mod3, gf, cw, wom, woc, wor, wmo, wrh, wrl, br, seq):
    t, d = x2.shape
    tm = min(ROW_TILE, seq)
    per_b = seq // tm
    row = lambda i: (i, 0)
    modspec = lambda k: pl.BlockSpec((1, 1, d), lambda i: ((i // per_b) * 6 + k, 0, 0))
    return pl.pallas_call(
        functools.partial(_mix_kernel, tm=tm, per_b=per_b),
        grid=(t // tm,),
        in_specs=[
            pl.BlockSpec((tm, GATE_W), row),
            pl.BlockSpec((tm, 3 * CONV_WIDTH), lambda i: (i, CONV_OFF // (3 * CONV_WIDTH))),
            pl.BlockSpec((tm, MLA_HEADS * MLA_V), row),
            pl.BlockSpec((tm, RET_HEADS * RET_DV), row),
            pl.BlockSpec((tm, d), row),
            modspec(2), modspec(3), modspec(4),
            _const_spec((1, d)),
            _const_spec((SUBLANES, CONV_WIDTH)),
            _const_spec(wom.shape), _const_spec(woc.shape), _const_spec(wor.shape), _const_spec(wmo.shape),
            _const_spec(wrh.shape), _const_spec(wrl.shape), _const_spec((1, LANES)),
        ],
        out_specs=[pl.BlockSpec((tm, d), row), pl.BlockSpec((tm * ROW_SPLIT, LANES), row),
                   pl.BlockSpec((SUBLANES, tm), lambda i: (0, i)), _const_spec((SUBLANES, LANES))],
        out_shape=[
            jax.ShapeDtypeStruct((t, d), F32),
            jax.ShapeDtypeStruct((t * ROW_SPLIT, LANES), F32),
            jax.ShapeDtypeStruct((SUBLANES, t), F32),
            jax.ShapeDtypeStruct((SUBLANES, LANES), F32),
        ],
        scratch_shapes=[pltpu.VMEM((tm + SUBLANES, CONV_WIDTH), F32)],
        compiler_params=_params(("arbitrary",)),
        name="mix_router",
    )(proj, proj, att, ret, x2, mod3, mod3, mod3, gf, cw, wom, woc, wor, wmo, wrh, wrl, br)


def _moe_kernel(te_ref, nv_ref, meta_ref, meta_next_ref, wt_ref, wg_ref, wu_ref, wd_ref, h_hbm, out_hbm,
                xbuf, ybuf, wgb, wub, wdb, gsem, ssem, *, tm):
    i = pl.program_id(0)
    nt = pl.num_programs(0)
    slot = i % 2
    nv = nv_ref[i]
    prev = jnp.maximum(i - 1, 0)
    prev2 = jnp.maximum(i - 2, 0)

    def gather_copy(tok, s, r):
        src = h_hbm.at[pl.ds(pl.multiple_of(tok * ROW_SPLIT, ROW_SPLIT), ROW_SPLIT)]
        return pltpu.make_async_copy(src, xbuf.at[pl.ds((s * tm + r) * ROW_SPLIT, ROW_SPLIT)], gsem.at[s])

    def scatter_copy(dst, s, r):
        dst_rows = out_hbm.at[pl.ds(pl.multiple_of(dst * ROW_SPLIT, ROW_SPLIT), ROW_SPLIT)]
        return pltpu.make_async_copy(ybuf.at[pl.ds((s * tm + r) * ROW_SPLIT, ROW_SPLIT)], dst_rows, ssem.at[s])

    def start_gather(meta, s):
        for r in range(tm):
            gather_copy(meta[0, 0, r], s, r).start(priority=r % 2)

    def wait_gather(s):
        for r in range(tm):
            gather_copy(0, s, r).wait()

    def start_scatter(s):
        for r in range(tm):
            scatter_copy(meta_ref[0, 0, tm + r], s, r).start(priority=r % 2)

    def wait_scatter(s):
        for r in range(tm):
            scatter_copy(0, s, r).wait()

    def for_slot(fn):
        @pl.when(slot == 0)
        def _():
            fn(0)

        @pl.when(slot == 1)
        def _():
            fn(1)

    @pl.when(i == 0)
    def _():
        ybuf[...] = jnp.zeros(ybuf.shape, F32)
        half = tm * ROW_SPLIT
        for s in range(2):
            pad_rows = out_hbm.at[pl.ds(out_hbm.shape[0] - (2 - s) * half, half)]
            init = pltpu.make_async_copy(ybuf.at[pl.ds(s * half, half)], pad_rows, ssem.at[s])
            init.start()
            init.wait()
        start_gather(meta_ref, 0)

    @pl.when((i >= 2) & (nv_ref[prev2] > 0))
    def _():
        for_slot(wait_scatter)

    @pl.when((i == 0) | (te_ref[i] != te_ref[prev]))
    def _():
        wgb[...] = wg_ref[0].astype(BF16)
        wub[...] = wu_ref[0].astype(BF16)
        wdb[...] = wd_ref[0].astype(BF16)

    def active(s):
        wait_gather(s)
        x = _load_token_tiles(xbuf, s * tm * ROW_SPLIT, tm).astype(BF16)
        start_gather(meta_next_ref, 1 - s)
        g = jnp.dot(x, wgb[...], preferred_element_type=F32)
        u = jnp.dot(x, wub[...], preferred_element_type=F32)
        hid = (g * _sigmoid(g) * u).astype(BF16)
        w_col = jnp.broadcast_to(wt_ref[0], (LANES, tm)).T[:, 0:1]
        y = jnp.dot(hid, wdb[...], preferred_element_type=F32) * w_col
        _store_token_tiles(ybuf, s * tm * ROW_SPLIT, tm, y)
        start_scatter(s)

    @pl.when(nv > 0)
    def _():
        for_slot(active)

    @pl.when((i > 0) & (nv == 0) & (nv_ref[prev] > 0))
    def _():
        for_slot(wait_gather)

    @pl.when((i == nt - 1) & (nv_ref[prev] > 0))
    def _():
        for_slot(lambda s: wait_scatter(1 - s))


def _route_tables(rout, cnt, t, tm, nt):
    e = rout[0:2].astype(jnp.int32).reshape(-1)
    w = rout[2:4].reshape(-1)
    n_slot = 2 * t
    perm = jnp.argsort(e).astype(jnp.int32)
    counts = cnt[0, ROUTE_E_OFF:ROUTE_E_OFF + N_EXPERTS].astype(jnp.int32)
    nt_e = (counts + tm - 1) // tm
    tile_end = jnp.cumsum(nt_e)
    tile_start = tile_end - nt_e
    off = jnp.cumsum(counts) - counts
    total = tile_end[-1]
    i = jnp.arange(nt, dtype=jnp.int32)
    e_raw = jnp.minimum(jnp.sum((tile_end[None, :] <= i[:, None]).astype(jnp.int32), axis=1), N_EXPERTS - 1)
    e_i = jnp.where(i < total, e_raw, e_raw[jnp.maximum(total - 1, 0)])
    j = i - tile_start[e_i]
    nvalid = jnp.where(i < total, jnp.clip(counts[e_i] - j * tm, 0, tm), 0).astype(jnp.int32)
    r = jnp.arange(tm, dtype=jnp.int32)
    pos = jnp.clip((off[e_i] + j * tm)[:, None] + r[None, :], 0, n_slot - 1)
    slot = perm[pos]
    valid = r[None, :] < nvalid[:, None]
    tok = jnp.where(slot >= t, slot - t, slot)
    dst = jnp.where(valid, slot, n_slot + (i % 2)[:, None] * tm + r[None, :])
    wt = jnp.where(valid, w[slot], 0.0)
    meta = jnp.concatenate([tok, dst], axis=1).astype(jnp.int32).reshape(nt, 1, 2 * tm)
    return e_i, nvalid, meta, wt.reshape(nt, 1, tm)


def _moe(h, rout, cnt, wg, wu, wd, layer):
    t, d = h.shape[0] // ROW_SPLIT, D_MODEL
    tm = min(MOE_TILE, t)
    nt = (2 * t) // tm + N_EXPERTS
    e_i, nvalid, meta, wt = _route_tables(rout, cnt, t, tm, nt)
    f = wg.shape[-1]
    grid_spec = pltpu.PrefetchScalarGridSpec(
        num_scalar_prefetch=2,
        grid=(nt,),
        in_specs=[
            pl.BlockSpec((1, 1, 2 * tm), lambda i, te, nv: (i, 0, 0), memory_space=pltpu.SMEM),
            pl.BlockSpec((1, 1, 2 * tm), lambda i, te, nv: (jnp.minimum(i + 1, nt - 1), 0, 0), memory_space=pltpu.SMEM),
            pl.BlockSpec((1, 1, tm), lambda i, te, nv: (i, 0, 0)),
            pl.BlockSpec((1, 1, d, f), lambda i, te, nv: (layer, te[i], 0, 0)),
            pl.BlockSpec((1, 1, d, f), lambda i, te, nv: (layer, te[i], 0, 0)),
            pl.BlockSpec((1, 1, f, d), lambda i, te, nv: (layer, te[i], 0, 0)),
            pl.BlockSpec(memory_space=pl.ANY),
        ],
        out_specs=pl.BlockSpec(memory_space=pl.ANY),
        scratch_shapes=[
            pltpu.VMEM((2 * tm * ROW_SPLIT, LANES), F32),
            pltpu.VMEM((2 * tm * ROW_SPLIT, LANES), F32),
            pltpu.VMEM((d, f), BF16),
            pltpu.VMEM((d, f), BF16),
            pltpu.VMEM((f, d), BF16),
            pltpu.SemaphoreType.DMA((2,)),
            pltpu.SemaphoreType.DMA((2,)),
        ],
    )

    def kern(te_ref, nv_ref, meta_ref, meta_next_ref, wt_ref, wg_ref, wu_ref, wd_ref, h_hbm, out_hbm, *scratch):
        _moe_kernel(te_ref, nv_ref, meta_ref, meta_next_ref, wt_ref, wg_ref.at[0], wu_ref.at[0], wd_ref.at[0],
                    h_hbm, out_hbm, *scratch, tm=tm)

    return pl.pallas_call(
        kern,
        grid_spec=grid_spec,
        out_shape=jax.ShapeDtypeStruct(((2 * t + 2 * tm) * ROW_SPLIT, LANES), F32),
        compiler_params=_params(("arbitrary",)),
        name="moe_experts",
    )(e_i, nvalid, meta, meta, wt, wg, wu, wd, h)


def _combine_kernel(x_ref, a_ref, b_ref, gt_ref, g_ref, o_ref, *, final):
    tm = x_ref.shape[0]
    moe = _load_token_tiles(a_ref, 0, tm) + _load_token_tiles(b_ref, 0, tm)
    x = x_ref[...] + gt_ref[0] * moe
    if final:
        x = _rms(x, g_ref[...])
    o_ref[...] = x


def _combine(x2, moe_out, mod3, final_g, seq, final):
    t, d = x2.shape
    tm = min(512, seq)
    per_b = seq // tm
    return pl.pallas_call(
        functools.partial(_combine_kernel, final=final),
        grid=(t // tm,),
        in_specs=[
            pl.BlockSpec((tm, d), lambda i: (i, 0)),
            pl.BlockSpec((tm * ROW_SPLIT, LANES), lambda i: (i, 0)),
            pl.BlockSpec((tm * ROW_SPLIT, LANES), lambda i: (t // tm + i, 0)),
            pl.BlockSpec((1, 1, d), lambda i: ((i // per_b) * 6 + 5, 0, 0)),
            _const_spec((1, d)),
        ],
        out_specs=pl.BlockSpec((tm, d), lambda i: (i, 0)),
        out_shape=jax.ShapeDtypeStruct((t, d), F32),
        compiler_params=_params(("arbitrary",)),
        name="combine",
    )(x2, moe_out, moe_out, mod3, final_g)


def _prep_w_in(w):
    d = w.shape[0]
    sizes = (MLA_Q_RANK, MLA_KV_RANK, MLA_ROPE, CONV_WIDTH, CONV_WIDTH, CONV_WIDTH,
             RET_HEADS * RET_DK, RET_HEADS * RET_DK, RET_HEADS * RET_DV, RET_HEADS * RET_DV, 3 * D_MODEL)
    parts = []
    start = 0
    for size in sizes:
        parts.append(w[:, start:start + size])
        start += size
    q_lat, kv_lat, kr, cb, cc, cx, rq, rk, rv, rg, gl = parts
    half = MLA_ROPE // 2
    zeros = lambda n: jnp.zeros((d, n), w.dtype)
    kblock = jnp.concatenate(
        [-kr[:, half:], kr[:, :half], zeros(MLA_NOPE - MLA_ROPE), kr, zeros(LANES - MLA_NOPE - MLA_ROPE)], axis=1)
    out = jnp.concatenate([gl, rq, rk, rv, rg, cb, cc, cx, q_lat, kv_lat, kblock], axis=1)
    return out.astype(BF16)


def _prep_w_uq(w):
    r = w.shape[0]
    dq = MLA_NOPE + MLA_ROPE
    half = MLA_ROPE // 2
    w3 = w.reshape(r, MLA_HEADS, dq)
    pad = QPAD - dq
    wa = jnp.pad(w3, ((0, 0), (0, 0), (0, pad))).reshape(r, MLA_HEADS * QPAD)
    wb = jnp.concatenate(
        [jnp.zeros((r, MLA_HEADS, MLA_NOPE), w.dtype), -w3[:, :, MLA_NOPE + half:], w3[:, :, MLA_NOPE:MLA_NOPE + half],
         jnp.zeros((r, MLA_HEADS, pad), w.dtype)], axis=2).reshape(r, MLA_HEADS * QPAD)
    return wa.astype(BF16), wb.astype(BF16)


def _prep_w_ukv(w):
    r = w.shape[0]
    w3 = w.reshape(r, MLA_HEADS, MLA_NOPE + MLA_V)
    wk = jnp.pad(w3[:, :, :MLA_NOPE], ((0, 0), (0, 0), (0, QPAD - MLA_NOPE))).reshape(r, MLA_HEADS * QPAD)
    wv = jnp.pad(w3[:, :, MLA_NOPE:], ((0, 0), (0, 0), (0, QPAD - MLA_V))).reshape(r, MLA_HEADS * QPAD)
    return wk.astype(BF16), wv.T.astype(BF16)


def _prep_router(w_rg, b_rg, w_re, b_re):
    d = w_rg.shape[0]
    w = jnp.zeros((d, LANES), F32)
    w = w.at[:, :N_GROUPS].set(w_rg).at[:, ROUTE_E_OFF:ROUTE_E_OFF + N_EXPERTS].set(w_re)
    b = jnp.zeros((1, LANES), F32)
    b = b.at[0, :N_GROUPS].set(b_rg).at[0, ROUTE_E_OFF:ROUTE_E_OFF + N_EXPERTS].set(b_re)
    w_hi = w.astype(BF16)
    w_lo = (w - w_hi.astype(F32)).astype(BF16)
    return w_hi, w_lo, b


def kernel(x, c, positions, w_ada, b_ada, norm_mix_g, norm_ffn_g, w_in, mla_q_norm_g, mla_kv_norm_g, w_uq, w_ukv, w_o_mla, conv_w, w_o_conv, w_o_ret, w_mix_out, w_route_group, b_route_group, w_route_expert, b_route_expert, w_exp_gate, w_exp_up, w_exp_down, final_g):
    batch, seq, d = x.shape
    depth = w_in.shape[0]
    t = batch * seq
    x2 = x.reshape(t, d)
    mod = _ada_mod(c, w_ada, b_ada)
    rc, rs, mc, ms = _rope_tables(positions)
    for l in range(depth):
        mod3 = mod[l].reshape(batch * 6, 1, d)
        proj = _in_proj(x2, mod3, norm_mix_g[l].reshape(1, d), _prep_w_in(w_in[l]), seq)
        wa, wb = _prep_w_uq(w_uq[l])
        wk, wv = _prep_w_ukv(w_ukv[l])
        q, k, v = _mla_prep(proj, mc, ms, mla_q_norm_g[l].reshape(1, -1), mla_kv_norm_g[l].reshape(1, -1), wa, wb, wk, wv,
                            batch, seq)
        att = _attention(q, k, v, batch, seq)
        ret = _retention(proj, rc, rs, batch, seq)
        wrh, wrl, br = _prep_router(w_route_group[l], b_route_group[l], w_route_expert[l], b_route_expert[l])
        cw = jnp.pad(conv_w[l], ((0, SUBLANES - CONV_K), (0, 0)))
        x2, h, rout, cnt = _mix(proj, att, ret, x2, mod3, norm_ffn_g[l].reshape(1, d), cw,
                           w_o_mla[l].astype(BF16), w_o_conv[l].astype(BF16), w_o_ret[l].astype(BF16),
                           w_mix_out[l].astype(BF16), wrh, wrl, br, seq)
        moe_out = _moe(h, rout, cnt, w_exp_gate, w_exp_up, w_exp_down, l)
        x2 = _combine(x2, moe_out, mod3, final_g.reshape(1, d), seq, final=(l == depth - 1))
    return x2.reshape(batch, seq, d)
```

```python
import functools

import jax
import jax.numpy as jnp
from jax import lax
from jax.experimental import pallas as pl
from jax.experimental.pallas import tpu as pltpu

F32 = jnp.float32
BF16 = jnp.bfloat16

D_MODEL = 1024
CHUNK = 64
EPS = 1e-6
ROPE_THETA = 10000.0
LOG2E = 1.4426950408889634

MLA_HEADS = 8
MLA_Q_RANK = 384
MLA_KV_RANK = 256
MLA_NOPE = 64
MLA_ROPE = 32
MLA_V = 64

CONV_WIDTH = 512
CONV_K = 3

RET_HEADS = 4
RET_DK = 128
RET_DV = 256

N_GROUPS = 4
EXPERTS_PER_GROUP = 8
N_EXPERTS = N_GROUPS * EXPERTS_PER_GROUP
EXPERT_FF = 512

LANES = 128
SUBLANES = 8
VMEM_LIMIT = 52 * 1024 * 1024

GATE_W = 3 * D_MODEL
RQ_OFF = GATE_W
RK_OFF = RQ_OFF + RET_HEADS * RET_DK
RV_OFF = RK_OFF + RET_HEADS * RET_DK
RG_OFF = RV_OFF + RET_HEADS * RET_DV
CONV_OFF = RG_OFF + RET_HEADS * RET_DV
MLA_OFF = CONV_OFF + 3 * CONV_WIDTH
MLA_W = MLA_Q_RANK + MLA_KV_RANK + LANES
PROJ_W = MLA_OFF + MLA_W
PROJ_N_TILES = 3
PROJ_TN = PROJ_W // PROJ_N_TILES
assert PROJ_TN * PROJ_N_TILES == PROJ_W and PROJ_TN % LANES == 0
assert MLA_OFF % MLA_W == 0 and CONV_OFF % (3 * CONV_WIDTH) == 0

QPAD = LANES
ROUTE_E_OFF = 32

ROW_TILE_PROJ = 1024
ROW_TILE = 512
ATT_TILE = 512
ATT_SUB = 256
ATT_GROUP = 4
LAZY_MAX_HEADROOM = 60.0
RET_BLOCK = 256
MOE_TILE = 256
ADA_TN = 768


def _params(sem, vmem=VMEM_LIMIT):
    return pltpu.CompilerParams(dimension_semantics=sem, vmem_limit_bytes=vmem)


def _const_spec(shape):
    nd = len(shape)
    return pl.BlockSpec(shape, lambda *_: (0,) * nd)


ROW_SPLIT = D_MODEL // LANES
assert ROW_SPLIT == SUBLANES


def _store_token_tiles(ref, base, rows, val):
    for s in range(ROW_SPLIT):
        ref[pl.ds(base + s, rows, stride=ROW_SPLIT), :] = val[:, s * LANES:(s + 1) * LANES]


def _load_token_tiles(ref, base, rows):
    return jnp.concatenate([ref[pl.ds(base + s, rows, stride=ROW_SPLIT), :] for s in range(ROW_SPLIT)], axis=1)


def _rms(x, g):
    return x * lax.rsqrt(jnp.mean(x * x, axis=-1, keepdims=True) + EPS) * g


def _sigmoid(x):
    return 1.0 / (1.0 + jnp.exp(-x))


def _ada_kernel(ct_ref, w_ref, b_ref, o_ref):
    ct = ct_ref[...]
    a = ct * _sigmoid(ct)
    w = w_ref[0]
    rows = [jnp.sum(w * a[:, b:b + 1], axis=0, keepdims=True) for b in range(ct.shape[1])]
    o_ref[0] = jnp.concatenate(rows, axis=0) + b_ref[0]


def _ada_mod(c, w_ada, b_ada):
    depth, d, n = w_ada.shape
    b = c.shape[0]
    return pl.pallas_call(
        _ada_kernel,
        grid=(depth, n // ADA_TN),
        in_specs=[
            _const_spec((d, b)),
            pl.BlockSpec((1, d, ADA_TN), lambda l, j: (l, 0, j)),
            pl.BlockSpec((1, 1, ADA_TN), lambda l, j: (l, 0, j)),
        ],
        out_specs=pl.BlockSpec((1, b, ADA_TN), lambda l, j: (l, 0, j)),
        out_shape=jax.ShapeDtypeStruct((depth, b, n), F32),
        compiler_params=_params(("arbitrary", "arbitrary")),
        name="ada_mod",
    )(c.T, w_ada, b_ada.reshape(depth, 1, n))


def _rope_kernel(pos_ref, f_ref, rc_ref, rs_ref, mc_ref, ms_ref):
    ang = pos_ref[...].astype(F32) * f_ref[...]
    cs = jnp.cos(ang)
    sn = jnp.sin(ang)
    lane = lax.broadcasted_iota(jnp.int32, ang.shape, 1)
    half = LANES // 2
    lo = lane < half
    rc_ref[...] = jnp.where(lo, cs, pltpu.roll(cs, half, 1))
    rs_ref[...] = jnp.where(lo, -sn, pltpu.roll(sn, half, 1))
    rope = (lane >= MLA_NOPE) & (lane < MLA_NOPE + MLA_ROPE)
    mc_ref[...] = jnp.where(lane < MLA_NOPE, 1.0, jnp.where(rope, cs, 0.0))
    ms_ref[...] = jnp.where(rope, sn, 0.0)


def _rope_tables(positions):
    t = positions.size
    tm = min(2048, t)
    inv_ret = ROPE_THETA ** (-jnp.arange(0, RET_DK, 2, dtype=F32) / RET_DK)
    inv_mla = ROPE_THETA ** (-jnp.arange(0, MLA_ROPE, 2, dtype=F32) / MLA_ROPE)
    freqs = jnp.concatenate([inv_ret, inv_mla, inv_mla, jnp.zeros((LANES - 96,), F32)]).reshape(1, LANES)
    spec = pl.BlockSpec((tm, LANES), lambda i: (i, 0))
    return pl.pallas_call(
        _rope_kernel,
        grid=(t // tm,),
        in_specs=[pl.BlockSpec((tm, 1), lambda i: (i, 0)), _const_spec((1, LANES))],
        out_specs=[spec] * 4,
        out_shape=[jax.ShapeDtypeStruct((t, LANES), F32)] * 4,
        compiler_params=_params(("arbitrary",)),
        name="rope_tables",
    )(positions.reshape(t, 1), freqs)


def _proj_kernel(x_ref, sh_ref, sc_ref, g_ref, w_ref, o_ref, h_scr):
    @pl.when(pl.program_id(1) == 0)
    def _():
        h = _rms(x_ref[...], g_ref[...]) * (1.0 + sc_ref[0]) + sh_ref[0]
        h_scr[...] = h.astype(BF16)

    o_ref[...] = jnp.dot(h_scr[...], w_ref[0], preferred_element_type=F32).astype(BF16)


def _in_proj(x2, mod3, g, w_all, layer, seq):
    t, d = x2.shape
    tm = min(ROW_TILE_PROJ, seq)
    per_b = seq // tm
    return pl.pallas_call(
        _proj_kernel,
        grid=(t // tm, PROJ_W // PROJ_TN),
        in_specs=[
            pl.BlockSpec((tm, d), lambda i, j: (i, 0)),
            pl.BlockSpec((1, 1, d), lambda i, j: ((i // per_b) * 6 + 0, 0, 0)),
            pl.BlockSpec((1, 1, d), lambda i, j: ((i // per_b) * 6 + 1, 0, 0)),
            _const_spec((1, d)),
            pl.BlockSpec((1, d, PROJ_TN), lambda i, j: (layer, 0, j)),
        ],
        out_specs=pl.BlockSpec((tm, PROJ_TN), lambda i, j: (i, j)),
        out_shape=jax.ShapeDtypeStruct((t, PROJ_W), BF16),
        scratch_shapes=[pltpu.VMEM((tm, d), BF16)],
        compiler_params=_params(("arbitrary", "arbitrary")),
        name="in_proj",
    )(x2, mod3, mod3, g, w_all)


def _mla_prep_kernel(in_ref, mc_ref, ms_ref, gq_ref, gkv_ref, wa_ref, wb_ref, wk_ref, wv_ref, q_ref, k_ref, v_ref):
    blk = in_ref[...].astype(F32)
    qn = _rms(blk[:, :MLA_Q_RANK], gq_ref[...]).astype(BF16)
    kvn = _rms(blk[:, MLA_Q_RANK:MLA_Q_RANK + MLA_KV_RANK], gkv_ref[...]).astype(BF16)
    kb = blk[:, MLA_Q_RANK + MLA_KV_RANK:]
    mc = mc_ref[...]
    ms = ms_ref[...]
    qa = jnp.dot(qn, wa_ref[...], preferred_element_type=F32)
    qb = jnp.dot(qn, wb_ref[...], preferred_element_type=F32)
    kn = jnp.dot(kvn, wk_ref[...], preferred_element_type=F32)
    lane = lax.broadcasted_iota(jnp.int32, kb.shape, 1)
    rope = (lane >= MLA_NOPE) & (lane < MLA_NOPE + MLA_ROPE)
    kpe = jnp.where(rope, kb * mc + pltpu.roll(kb, LANES // 2, 1) * ms, 0.0)
    scale = (MLA_NOPE + MLA_ROPE) ** -0.5 * LOG2E
    for h in range(MLA_HEADS):
        sl = slice(h * QPAD, (h + 1) * QPAD)
        q_ref[:, sl] = ((qa[:, sl] * mc + qb[:, sl] * ms) * scale).astype(BF16)
        k_ref[:, sl] = (kn[:, sl] + kpe).astype(BF16)
    vt = lax.dot_general(wv_ref[...], kvn, (((1,), (1,)), ((), ())), preferred_element_type=F32)
    row = lax.broadcasted_iota(jnp.int32, vt.shape, 0)
    v_ref[0, 0] = jnp.where(row % QPAD >= MLA_V, 1.0, vt).astype(BF16)


def _mla_prep(proj, mc, ms, gq, gkv, wa, wb, wk, wv, batch, seq):
    t = proj.shape[0]
    tm = min(ATT_TILE, seq)
    per_b = seq // tm
    hq = MLA_HEADS * QPAD
    hv = MLA_HEADS * QPAD
    return pl.pallas_call(
        _mla_prep_kernel,
        grid=(t // tm,),
        in_specs=[
            pl.BlockSpec((tm, MLA_W), lambda i: (i, MLA_OFF // MLA_W)),
            pl.BlockSpec((tm, LANES), lambda i: (i, 0)),
            pl.BlockSpec((tm, LANES), lambda i: (i, 0)),
            _const_spec((1, MLA_Q_RANK)),
            _const_spec((1, MLA_KV_RANK)),
            _const_spec((MLA_Q_RANK, hq)),
            _const_spec((MLA_Q_RANK, hq)),
            _const_spec((MLA_KV_RANK, hq)),
            _const_spec((hv, MLA_KV_RANK)),
        ],
        out_specs=[
            pl.BlockSpec((tm, hq), lambda i: (i, 0)),
            pl.BlockSpec((tm, hq), lambda i: (i, 0)),
            pl.BlockSpec((1, 1, hv, tm), lambda i: (i // per_b, i % per_b, 0, 0)),
        ],
        out_shape=[
            jax.ShapeDtypeStruct((t, hq), BF16),
            jax.ShapeDtypeStruct((t, hq), BF16),
            jax.ShapeDtypeStruct((batch, per_b, hv, tm), BF16),
        ],
        compiler_params=_params(("arbitrary",)),
        name="mla_prep",
    )(proj, mc, ms, gq, gkv, wa, wb, wk, wv)


def _attn_kernel(q_ref, k_ref, vt_ref, o_ref, m_scr, acc_scr, *, tile):
    qi = pl.program_id(2)
    nn_dims = (((1,), (1,)), ((), ()))

    def head_q(a):
        return q_ref[0, :, a * QPAD:(a + 1) * QPAD]

    def head_k(start, size, a):
        return k_ref[0, pl.ds(start, size), a * QPAD:(a + 1) * QPAD]

    def exact_step(j, src, dst, first, masked):
        start = pl.multiple_of(j * tile, tile)
        for a in range(2):
            st = lax.dot_general(head_k(start, tile, a), head_q(a), nn_dims, preferred_element_type=F32)
            if masked:
                key = lax.broadcasted_iota(jnp.int32, (tile, tile), 0)
                qry = lax.broadcasted_iota(jnp.int32, (tile, tile), 1)
                st = jnp.where((key // CHUNK) <= (qry // CHUNK), st, -jnp.inf)
            m_new = jnp.max(st, axis=0, keepdims=True)
            if not first:
                m_old = m_scr[src, a]
                m_new = jnp.maximum(m_old, m_new)
            p = jnp.exp2(st - m_new).astype(BF16)
            vt = vt_ref[0, j, a * QPAD:(a + 1) * QPAD, :]
            pv = jnp.dot(vt, p, preferred_element_type=F32)
            if not first:
                pv = pv + jnp.exp2(m_old - m_new) * acc_scr[src, a]
            acc_scr[dst, a] = pv
            m_scr[dst, a] = m_new

    exact_step(qi, 0, 0, first=True, masked=True)

    def lazy_step(j0, n_tiles, cur):
        nxt = 1 - cur
        n_sub = tile // ATT_SUB
        scores = {}
        for a in range(2):
            for t in range(n_tiles):
                start = pl.multiple_of((j0 + t) * tile, tile)
                for s in range(n_sub):
                    k = head_k(start + s * ATT_SUB, ATT_SUB, a)
                    scores[a, t, s] = lax.dot_general(k, head_q(a), nn_dims, preferred_element_type=F32)
        excess = jnp.zeros((1, tile), F32)
        for a in range(2):
            m_used = m_scr[cur, a]
            m_new = m_used
            pv = jnp.zeros((QPAD, tile), F32)
            for t in range(n_tiles):
                for s in range(n_sub):
                    st = scores[a, t, s]
                    m_new = jnp.maximum(m_new, jnp.max(st, axis=0, keepdims=True))
                    p = jnp.exp2(st - m_used).astype(BF16)
                    vt = vt_ref[0, j0 + t, a * QPAD:(a + 1) * QPAD, s * ATT_SUB:(s + 1) * ATT_SUB]
                    pv = pv + jnp.dot(vt, p, preferred_element_type=F32)
            acc_scr[nxt, a] = (acc_scr[cur, a] + pv) * jnp.exp2(m_used - m_new)
            m_scr[nxt, a] = m_new
            excess = jnp.maximum(excess, m_new - m_used)

        @pl.when(jnp.max(excess) > LAZY_MAX_HEADROOM)
        def _():
            def redo(t, carry):
                exact_step(j0 + t, jnp.where(t == 0, cur, nxt), nxt, first=False, masked=False)
                return carry

            lax.fori_loop(0, n_tiles, redo, 0)

        return nxt

    n_groups = qi // ATT_GROUP
    cur = lax.fori_loop(0, n_groups, lambda g, c: lazy_step(g * ATT_GROUP, ATT_GROUP, c), 0)
    done = n_groups * ATT_GROUP
    size = ATT_GROUP // 2
    while size >= 1:
        take = (qi - done) >= size

        @pl.when(take)
        def _(done=done, size=size, cur=cur):
            lazy_step(done, size, cur)

        cur = jnp.where(take, 1 - cur, cur)
        done = jnp.where(take, done + size, done)
        size //= 2

    lane = lax.broadcasted_iota(jnp.int32, (tile, LANES), 1)
    r0 = acc_scr[cur, 0].T
    r1 = acc_scr[cur, 1].T
    o0 = r0 / r0[:, MLA_V:MLA_V + 1]
    o1 = pltpu.roll(r1, MLA_V, 1) / r1[:, MLA_V:MLA_V + 1]
    o_ref[0] = jnp.where(lane < MLA_V, o0, o1).astype(BF16)


def _attention(q, k, vt, batch, seq):
    tile = min(ATT_TILE, seq)
    q3 = q.reshape(batch, seq, MLA_HEADS * QPAD)
    k3 = k.reshape(batch, seq, MLA_HEADS * QPAD)
    out = pl.pallas_call(
        functools.partial(_attn_kernel, tile=tile),
        grid=(batch, MLA_HEADS // 2, seq // tile),
        in_specs=[
            pl.BlockSpec((1, tile, 2 * QPAD), lambda b, h, i: (b, i, h)),
            pl.BlockSpec((1, seq, 2 * QPAD), lambda b, h, i: (b, 0, h)),
            pl.BlockSpec((1, seq // tile, 2 * QPAD, tile), lambda b, h, i: (b, 0, h, 0)),
        ],
        out_specs=pl.BlockSpec((1, tile, 2 * MLA_V), lambda b, h, i: (b, i, h)),
        out_shape=jax.ShapeDtypeStruct((batch, seq, MLA_HEADS * MLA_V), BF16),
        scratch_shapes=[
            pltpu.VMEM((2, 2, 1, tile), F32),
            pltpu.VMEM((2, 2, QPAD, tile), F32),
        ],
        compiler_params=_params(("arbitrary", "arbitrary", "arbitrary")),
        name="mla_attention",
    )(q3, k3, vt)
    return out.reshape(batch * seq, MLA_HEADS * MLA_V)


def _ret_kernel(lg_ref, q_ref, k_ref, v_ref, g_ref, rc_ref, rs_ref, o_ref, state_scr, decay_scr, *, blk):
    n = pl.program_id(1)

    @pl.when(n == 0)
    def _():
        state_scr[...] = jnp.zeros(state_scr.shape, F32)
        row = lax.broadcasted_iota(jnp.int32, (blk, blk), 0)
        col = lax.broadcasted_iota(jnp.int32, (blk, blk), 1)
        dist = jnp.abs(row - col).astype(F32)
        visible = (col // CHUNK) <= (row // CHUNK)
        for h in range(RET_HEADS):
            decay_scr[h] = jnp.where(visible, jnp.exp(lg_ref[h] * dist), 0.0)

    rc = rc_ref[...]
    rs = rs_ref[...]
    half = RET_DK // 2
    pos = lax.broadcasted_iota(jnp.int32, (blk, 1), 0).astype(F32)
    for h in range(RET_HEADS):
        lg = lg_ref[h]
        ksl = slice(h * RET_DK, (h + 1) * RET_DK)
        vsl = slice(h * RET_DV, (h + 1) * RET_DV)
        qf = q_ref[:, ksl].astype(F32)
        kf = k_ref[:, ksl].astype(F32)
        q = qf * rc + pltpu.roll(qf, half, 1) * rs
        k = (kf * rc + pltpu.roll(kf, half, 1) * rs) * (RET_DK ** -0.5)
        v = v_ref[:, vsl]
        q_dec = jnp.exp(lg * (pos + 1.0))
        k_dec = jnp.exp(lg * (blk - 1.0 - pos))

        s = lax.dot_general(q.astype(BF16), k.astype(BF16), (((1,), (1,)), ((), ())), preferred_element_type=F32)
        s = s * decay_scr[h]
        state = state_scr[h]
        o = jnp.dot(s.astype(BF16), v, preferred_element_type=F32)
        o = o + jnp.dot((q * q_dec).astype(BF16), state.astype(BF16), preferred_element_type=F32)
        kd = (k * k_dec).astype(BF16)
        upd = lax.dot_general(kd, v, (((0,), (0,)), ((), ())), preferred_element_type=F32)
        state_scr[h] = state * jnp.exp(lg * blk) + upd

        o = o * lax.rsqrt(jnp.mean(o * o, axis=-1, keepdims=True) + EPS)
        g = g_ref[:, vsl].astype(F32)
        o_ref[:, vsl] = (g * _sigmoid(g) * o).astype(BF16)


def _retention(proj, rc, rs, batch, seq):
    t = proj.shape[0]
    blk = min(RET_BLOCK, seq)
    per_b = seq // blk
    log_gamma = jnp.log(1.0 - 2.0 ** (-5.0 - jnp.arange(RET_HEADS, dtype=F32)))
    qk_w = RET_HEADS * RET_DK
    v_w = RET_HEADS * RET_DV
    row = lambda b, n: b * per_b + n
    return pl.pallas_call(
        functools.partial(_ret_kernel, blk=blk),
        grid=(batch, per_b),
        in_specs=[
            pl.BlockSpec(memory_space=pltpu.SMEM),
            pl.BlockSpec((blk, qk_w), lambda b, n: (row(b, n), RQ_OFF // qk_w)),
            pl.BlockSpec((blk, qk_w), lambda b, n: (row(b, n), RK_OFF // qk_w)),
            pl.BlockSpec((blk, v_w), lambda b, n: (row(b, n), RV_OFF // v_w)),
            pl.BlockSpec((blk, v_w), lambda b, n: (row(b, n), RG_OFF // v_w)),
            pl.BlockSpec((blk, LANES), lambda b, n: (row(b, n), 0)),
            pl.BlockSpec((blk, LANES), lambda b, n: (row(b, n), 0)),
        ],
        out_specs=pl.BlockSpec((blk, v_w), lambda b, n: (row(b, n), 0)),
        out_shape=jax.ShapeDtypeStruct((t, v_w), BF16),
        scratch_shapes=[pltpu.VMEM((RET_HEADS, RET_DK, RET_DV), F32), pltpu.VMEM((RET_HEADS, blk, blk), F32)],
        compiler_params=_params(("arbitrary", "arbitrary")),
        name="retention",
    )(log_gamma, proj, proj, proj, proj, rc, rs)


def _mix_kernel(gate_ref, conv_ref, att_ref, ret_ref, x_ref, gt_ref, sh_ref, sc_ref, gf_ref, cw_ref,
                wom_ref, woc_ref, wor_ref, wmo_ref, wrh_ref, wrl_ref, br_ref,
                xo_ref, h_ref, rout_ref, cnt_ref, u_scr, *, tm, per_b):
    i = pl.program_id(0)
    d = D_MODEL
    cw = CONV_WIDTH

    cb = conv_ref[:, 0:cw].astype(F32)
    u = conv_ref[:, cw:2 * cw].astype(F32) * conv_ref[:, 2 * cw:3 * cw].astype(F32)

    @pl.when(i % per_b == 0)
    def _():
        u_scr[0:SUBLANES, :] = jnp.zeros((SUBLANES, cw), F32)

    @pl.when(i % per_b != 0)
    def _():
        u_scr[0:SUBLANES, :] = u_scr[tm:tm + SUBLANES, :]

    u_scr[SUBLANES:SUBLANES + tm, :] = u
    y = cw_ref[2:3, :] * u
    y = y + cw_ref[1:2, :] * u_scr[SUBLANES - 1:SUBLANES - 1 + tm, :]
    y = y + cw_ref[0:1, :] * u_scr[SUBLANES - 2:SUBLANES - 2 + tm, :]
    z = (cb * y).astype(BF16)

    y_mla = jnp.dot(att_ref[...], wom_ref[...], preferred_element_type=F32)
    y_conv = jnp.dot(z, woc_ref[...], preferred_element_type=F32)
    y_ret = jnp.dot(ret_ref[...], wor_ref[...], preferred_element_type=F32)
    merged = _sigmoid(gate_ref[:, 0:d].astype(F32)) * y_mla
    merged = merged + _sigmoid(gate_ref[:, d:2 * d].astype(F32)) * y_conv
    merged = merged + _sigmoid(gate_ref[:, 2 * d:3 * d].astype(F32)) * y_ret
    x = x_ref[...] + gt_ref[0] * jnp.dot(merged.astype(BF16), wmo_ref[...], preferred_element_type=F32)
    xo_ref[...] = x

    h = _rms(x, gf_ref[...]) * (1.0 + sc_ref[0]) + sh_ref[0]
    _store_token_tiles(h_ref, 0, tm, h)

    h_hi = h.astype(BF16)
    h_lo = (h - h_hi.astype(F32)).astype(BF16)
    lg = jnp.dot(h_hi, wrh_ref[...], preferred_element_type=F32)
    lg = lg + jnp.dot(h_lo, wrh_ref[...], preferred_element_type=F32)
    lg = lg + jnp.dot(h_hi, wrl_ref[...], preferred_element_type=F32)
    lg = lg + br_ref[...]

    lane = lax.broadcasted_iota(jnp.int32, lg.shape, 1)
    neg = -jnp.inf
    gl = jnp.where(lane < N_GROUPS, lg, neg)
    gmax = jnp.max(gl, axis=-1, keepdims=True)
    g_val = 1.0 / jnp.sum(jnp.exp(gl - gmax), axis=-1, keepdims=True)
    g_idx = jnp.min(jnp.where(gl == gmax, lane, LANES), axis=-1, keepdims=True)
    in_group = (lane >= ROUTE_E_OFF) & (lane < ROUTE_E_OFF + N_EXPERTS)
    in_group = in_group & (((lane - ROUTE_E_OFF) // EXPERTS_PER_GROUP) == g_idx)
    el = jnp.where(in_group, lg, neg)
    m1 = jnp.max(el, axis=-1, keepdims=True)
    i1 = jnp.min(jnp.where(el == m1, lane, LANES), axis=-1, keepdims=True)
    el2 = jnp.where(lane == i1, neg, el)
    m2 = jnp.max(el2, axis=-1, keepdims=True)
    i2 = jnp.min(jnp.where(el2 == m2, lane, LANES), axis=-1, keepdims=True)
    r = jnp.exp(m2 - m1)
    w1 = g_val / (1.0 + r)
    w2 = g_val * r / (1.0 + r)
    e1 = (i1 - ROUTE_E_OFF).astype(F32)
    e2 = (i2 - ROUTE_E_OFF).astype(F32)
    rout = jnp.where(lane == 0, e1, jnp.where(lane == 1, e2, jnp.where(lane == 2, w1, jnp.where(lane == 3, w2, 0.0))))
    rout_ref[...] = rout.T[0:SUBLANES, :]

    hits = jnp.where(lane == i1, 1.0, 0.0) + jnp.where(lane == i2, 1.0, 0.0)

    @pl.when(i == 0)
    def _():
        cnt_ref[...] = jnp.zeros(cnt_ref.shape, F32)

    cnt_ref[0:1, :] = cnt_ref[0:1, :] + jnp.sum(hits, axis=0, keepdims=True)


def _mix(proj, att, ret, x2, mod3, gf, cw, wom, woc, wor, wmo, wrh, wrl, br, seq):
    t, d = x2.shape
    tm = min(ROW_TILE, seq)
    per_b = seq // tm
    row = lambda i: (i, 0)
    modspec = lambda k: pl.BlockSpec((1, 1, d), lambda i: ((i // per_b) * 6 + k, 0, 0))
    return pl.pallas_call(
        functools.partial(_mix_kernel, tm=tm, per_b=per_b),
        grid=(t // tm,),
        in_specs=[
            pl.BlockSpec((tm, GATE_W), row),
            pl.BlockSpec((tm, 3 * CONV_WIDTH), lambda i: (i, CONV_OFF // (3 * CONV_WIDTH))),
            pl.BlockSpec((tm, MLA_HEADS * MLA_V), row),
            pl.BlockSpec((tm, RET_HEADS * RET_DV), row),
            pl.BlockSpec((tm, d), row),
            modspec(2), modspec(3), modspec(4),
            _const_spec((1, d)),
            _const_spec((SUBLANES, CONV_WIDTH)),
            _const_spec(wom.shape), _const_spec(woc.shape), _const_spec(wor.shape), _const_spec(wmo.shape),
            _const_spec(wrh.shape), _const_spec(wrl.shape), _const_spec((1, LANES)),
        ],
        out_specs=[pl.BlockSpec((tm, d), row), pl.BlockSpec((tm * ROW_SPLIT, LANES), row),
                   pl.BlockSpec((SUBLANES, tm), lambda i: (0, i)), _const_spec((SUBLANES, LANES))],
        out_shape=[
            jax.ShapeDtypeStruct((t, d), F32),
            jax.ShapeDtypeStruct((t * ROW_SPLIT, LANES), F32),
            jax.ShapeDtypeStruct((SUBLANES, t), F32),
            jax.ShapeDtypeStruct((SUBLANES, LANES), F32),
        ],
        scratch_shapes=[pltpu.VMEM((tm + SUBLANES, CONV_WIDTH), F32)],
        compiler_params=_params(("arbitrary",)),
        name="mix_router",
    )(proj, proj, att, ret, x2, mod3, mod3, mod3, gf, cw, wom, woc, wor, wmo, wrh, wrl, br)


def _moe_kernel(te_ref, nv_ref, meta_ref, meta_next_ref, wt_ref, wg_ref, wu_ref, wd_ref, h_hbm, out_hbm,
                xbuf, ybuf, wgb, wub, wdb, gsem, ssem, *, tm):
    i = pl.program_id(0)
    nt = pl.num_programs(0)
    slot = i % 2
    nv = nv_ref[i]
    prev = jnp.maximum(i - 1, 0)
    prev2 = jnp.maximum(i - 2, 0)

    def gather_copy(tok, s, r):
        src = h_hbm.at[pl.ds(pl.multiple_of(tok * ROW_SPLIT, ROW_SPLIT), ROW_SPLIT)]
        return pltpu.make_async_copy(src, xbuf.at[pl.ds((s * tm + r) * ROW_SPLIT, ROW_SPLIT)], gsem.at[s])

    def scatter_copy(dst, s, r):
        dst_rows = out_hbm.at[pl.ds(pl.multiple_of(dst * ROW_SPLIT, ROW_SPLIT), ROW_SPLIT)]
        return pltpu.make_async_copy(ybuf.at[pl.ds((s * tm + r) * ROW_SPLIT, ROW_SPLIT)], dst_rows, ssem.at[s])

    def start_gather(meta, s):
        for r in range(tm):
            gather_copy(meta[0, 0, r], s, r).start(priority=r % 2)

    def wait_gather(s):
        for r in range(tm):
            gather_copy(0, s, r).wait()

    def start_scatter(s):
        for r in range(tm):
            scatter_copy(meta_ref[0, 0, tm + r], s, r).start(priority=r % 2)

    def wait_scatter(s):
        for r in range(tm):
            scatter_copy(0, s, r).wait()

    def for_slot(fn):
        @pl.when(slot == 0)
        def _():
            fn(0)

        @pl.when(slot == 1)
        def _():
            fn(1)

    @pl.when(i == 0)
    def _():
        ybuf[...] = jnp.zeros(ybuf.shape, F32)
        half = tm * ROW_SPLIT
        for s in range(2):
            pad_rows = out_hbm.at[pl.ds(out_hbm.shape[0] - (2 - s) * half, half)]
            init = pltpu.make_async_copy(ybuf.at[pl.ds(s * half, half)], pad_rows, ssem.at[s])
            init.start()
            init.wait()
        start_gather(meta_ref, 0)

    @pl.when((i >= 2) & (nv_ref[prev2] > 0))
    def _():
        for_slot(wait_scatter)

    @pl.when((i == 0) | (te_ref[i] != te_ref[prev]))
    def _():
        wgb[...] = wg_ref[0].astype(BF16)
        wub[...] = wu_ref[0].astype(BF16)
        wdb[...] = wd_ref[0].astype(BF16)

    def active(s):
        wait_gather(s)
        x = _load_token_tiles(xbuf, s * tm * ROW_SPLIT, tm).astype(BF16)
        start_gather(meta_next_ref, 1 - s)
        g = jnp.dot(x, wgb[...], preferred_element_type=F32)
        u = jnp.dot(x, wub[...], preferred_element_type=F32)
        hid = (g * _sigmoid(g) * u).astype(BF16)
        w_col = jnp.broadcast_to(wt_ref[0], (LANES, tm)).T[:, 0:1]
        y = jnp.dot(hid, wdb[...], preferred_element_type=F32) * w_col
        _store_token_tiles(ybuf, s * tm * ROW_SPLIT, tm, y)
        start_scatter(s)

    @pl.when(nv > 0)
    def _():
        for_slot(active)

    @pl.when((i > 0) & (nv == 0) & (nv_ref[prev] > 0))
    def _():
        for_slot(wait_gather)

    @pl.when((i == nt - 1) & (nv_ref[prev] > 0))
    def _():
        for_slot(lambda s: wait_scatter(1 - s))


def _route_tables(rout, cnt, t, tm, nt):
    e = rout[0:2].astype(jnp.int32).reshape(-1)
    w = rout[2:4].reshape(-1)
    n_slot = 2 * t
    perm = jnp.argsort(e).astype(jnp.int32)
    counts = cnt[0, ROUTE_E_OFF:ROUTE_E_OFF + N_EXPERTS].astype(jnp.int32)
    nt_e = (counts + tm - 1) // tm
    tile_end = jnp.cumsum(nt_e)
    tile_start = tile_end - nt_e
    off = jnp.cumsum(counts) - counts
    total = tile_end[-1]
    i = jnp.arange(nt, dtype=jnp.int32)
    i_act = jnp.minimum(i, total - 1)
    e_i = jnp.minimum(jnp.sum((tile_end[None, :] <= i_act[:, None]).astype(jnp.int32), axis=1), N_EXPERTS - 1)
    sel = e_i[:, None] == jnp.arange(N_EXPERTS, dtype=jnp.int32)[None, :]
    pick = lambda v: jnp.sum(jnp.where(sel, v[None, :], 0), axis=1)
    j = i - pick(tile_start)
    nvalid = jnp.where(i < total, jnp.clip(pick(counts) - j * tm, 0, tm), 0).astype(jnp.int32)
    r = jnp.arange(tm, dtype=jnp.int32)
    pos = jnp.clip((pick(off) + j * tm)[:, None] + r[None, :], 0, n_slot - 1)
    slot = perm[pos]
    valid = r[None, :] < nvalid[:, None]
    tok = jnp.where(slot >= t, slot - t, slot)
    dst = jnp.where(valid, slot, n_slot + (i % 2)[:, None] * tm + r[None, :])
    wt = jnp.where(valid, w[slot], 0.0)
    meta = jnp.concatenate([tok, dst], axis=1).astype(jnp.int32).reshape(nt, 1, 2 * tm)
    return e_i, nvalid, meta, wt.reshape(nt, 1, tm)


def _moe(h, rout, cnt, wg, wu, wd, layer):
    t, d = h.shape[0] // ROW_SPLIT, D_MODEL
    tm = min(MOE_TILE, t)
    nt = (2 * t) // tm + N_EXPERTS
    e_i, nvalid, meta, wt = _route_tables(rout, cnt, t, tm, nt)
    f = wg.shape[-1]
    grid_spec = pltpu.PrefetchScalarGridSpec(
        num_scalar_prefetch=2,
        grid=(nt,),
        in_specs=[
            pl.BlockSpec((1, 1, 2 * tm), lambda i, te, nv: (i, 0, 0), memory_space=pltpu.SMEM),
            pl.BlockSpec((1, 1, 2 * tm), lambda i, te, nv: (jnp.minimum(i + 1, nt - 1), 0, 0), memory_space=pltpu.SMEM),
            pl.BlockSpec((1, 1, tm), lambda i, te, nv: (i, 0, 0)),
            pl.BlockSpec((1, 1, d, f), lambda i, te, nv: (layer, te[i], 0, 0)),
            pl.BlockSpec((1, 1, d, f), lambda i, te, nv: (layer, te[i], 0, 0)),
            pl.BlockSpec((1, 1, f, d), lambda i, te, nv: (layer, te[i], 0, 0)),
            pl.BlockSpec(memory_space=pl.ANY),
        ],
        out_specs=pl.BlockSpec(memory_space=pl.ANY),
        scratch_shapes=[
            pltpu.VMEM((2 * tm * ROW_SPLIT, LANES), F32),
            pltpu.VMEM((2 * tm * ROW_SPLIT, LANES), F32),
            pltpu.VMEM((d, f), BF16),
            pltpu.VMEM((d, f), BF16),
            pltpu.VMEM((f, d), BF16),
            pltpu.SemaphoreType.DMA((2,)),
            pltpu.SemaphoreType.DMA((2,)),
        ],
    )

    def kern(te_ref, nv_ref, meta_ref, meta_next_ref, wt_ref, wg_ref, wu_ref, wd_ref, h_hbm, out_hbm, *scratch):
        _moe_kernel(te_ref, nv_ref, meta_ref, meta_next_ref, wt_ref, wg_ref.at[0], wu_ref.at[0], wd_ref.at[0],
                    h_hbm, out_hbm, *scratch, tm=tm)

    return pl.pallas_call(
        kern,
        grid_spec=grid_spec,
        out_shape=jax.ShapeDtypeStruct(((2 * t + 2 * tm) * ROW_SPLIT, LANES), F32),
        compiler_params=_params(("arbitrary",)),
        name="moe_experts",
    )(e_i, nvalid, meta, meta, wt, wg, wu, wd, h)


def _combine_kernel(x_ref, a_ref, b_ref, gt_ref, g_ref, o_ref, *, final):
    tm = x_ref.shape[0]
    moe = _load_token_tiles(a_ref, 0, tm) + _load_token_tiles(b_ref, 0, tm)
    x = x_ref[...] + gt_ref[0] * moe
    if final:
        x = _rms(x, g_ref[...])
    o_ref[...] = x


def _combine(x2, moe_out, mod3, final_g, seq, final):
    t, d = x2.shape
    tm = min(512, seq)
    per_b = seq // tm
    return pl.pallas_call(
        functools.partial(_combine_kernel, final=final),
        grid=(t // tm,),
        in_specs=[
            pl.BlockSpec((tm, d), lambda i: (i, 0)),
            pl.BlockSpec((tm * ROW_SPLIT, LANES), lambda i: (i, 0)),
            pl.BlockSpec((tm * ROW_SPLIT, LANES), lambda i: (t // tm + i, 0)),
            pl.BlockSpec((1, 1, d), lambda i: ((i // per_b) * 6 + 5, 0, 0)),
            _const_spec((1, d)),
        ],
        out_specs=pl.BlockSpec((tm, d), lambda i: (i, 0)),
        out_shape=jax.ShapeDtypeStruct((t, d), F32),
        compiler_params=_params(("arbitrary",)),
        name="combine",
    )(x2, moe_out, moe_out, mod3, final_g)


def _prep_w_in(w):
    sizes = (MLA_Q_RANK, MLA_KV_RANK, MLA_ROPE, CONV_WIDTH, CONV_WIDTH, CONV_WIDTH,
             RET_HEADS * RET_DK, RET_HEADS * RET_DK, RET_HEADS * RET_DV, RET_HEADS * RET_DV, 3 * D_MODEL)
    parts = []
    start = 0
    for size in sizes:
        parts.append(w[..., start:start + size])
        start += size
    q_lat, kv_lat, kr, cb, cc, cx, rq, rk, rv, rg, gl = parts
    half = MLA_ROPE // 2
    zeros = lambda n: jnp.zeros(w.shape[:-1] + (n,), w.dtype)
    kblock = jnp.concatenate(
        [-kr[..., half:], kr[..., :half], zeros(MLA_NOPE - MLA_ROPE), kr, zeros(LANES - MLA_NOPE - MLA_ROPE)], axis=-1)
    out = jnp.concatenate([gl, rq, rk, rv, rg, cb, cc, cx, q_lat, kv_lat, kblock], axis=-1)
    return out.astype(BF16)


def _prep_w_uq(w):
    r = w.shape[0]
    dq = MLA_NOPE + MLA_ROPE
    half = MLA_ROPE // 2
    w3 = w.reshape(r, MLA_HEADS, dq)
    pad = QPAD - dq
    wa = jnp.pad(w3, ((0, 0), (0, 0), (0, pad))).reshape(r, MLA_HEADS * QPAD)
    wb = jnp.concatenate(
        [jnp.zeros((r, MLA_HEADS, MLA_NOPE), w.dtype), -w3[:, :, MLA_NOPE + half:], w3[:, :, MLA_NOPE:MLA_NOPE + half],
         jnp.zeros((r, MLA_HEADS, pad), w.dtype)], axis=2).reshape(r, MLA_HEADS * QPAD)
    return wa.astype(BF16), wb.astype(BF16)


def _prep_w_ukv(w):
    r = w.shape[0]
    w3 = w.reshape(r, MLA_HEADS, MLA_NOPE + MLA_V)
    wk = jnp.pad(w3[:, :, :MLA_NOPE], ((0, 0), (0, 0), (0, QPAD - MLA_NOPE))).reshape(r, MLA_HEADS * QPAD)
    wv = jnp.pad(w3[:, :, MLA_NOPE:], ((0, 0), (0, 0), (0, QPAD - MLA_V))).reshape(r, MLA_HEADS * QPAD)
    return wk.astype(BF16), wv.T.astype(BF16)


def _prep_router(w_rg, b_rg, w_re, b_re):
    d = w_rg.shape[0]
    w = jnp.zeros((d, LANES), F32)
    w = w.at[:, :N_GROUPS].set(w_rg).at[:, ROUTE_E_OFF:ROUTE_E_OFF + N_EXPERTS].set(w_re)
    b = jnp.zeros((1, LANES), F32)
    b = b.at[0, :N_GROUPS].set(b_rg).at[0, ROUTE_E_OFF:ROUTE_E_OFF + N_EXPERTS].set(b_re)
    w_hi = w.astype(BF16)
    w_lo = (w - w_hi.astype(F32)).astype(BF16)
    return w_hi, w_lo, b


def kernel(x, c, positions, w_ada, b_ada, norm_mix_g, norm_ffn_g, w_in, mla_q_norm_g, mla_kv_norm_g, w_uq, w_ukv, w_o_mla, conv_w, w_o_conv, w_o_ret, w_mix_out, w_route_group, b_route_group, w_route_expert, b_route_expert, w_exp_gate, w_exp_up, w_exp_down, final_g):
    batch, seq, d = x.shape
    depth = w_in.shape[0]
    t = batch * seq
    x2 = x.reshape(t, d)
    mod = _ada_mod(c, w_ada, b_ada)
    rc, rs, mc, ms = _rope_tables(positions)
    w_in_all = _prep_w_in(w_in)
    for l in range(depth):
        mod3 = mod[l].reshape(batch * 6, 1, d)
        proj = _in_proj(x2, mod3, norm_mix_g[l].reshape(1, d), w_in_all, l, seq)
        wa, wb = _prep_w_uq(w_uq[l])
        wk, wv = _prep_w_ukv(w_ukv[l])
        q, k, v = _mla_prep(proj, mc, ms, mla_q_norm_g[l].reshape(1, -1), mla_kv_norm_g[l].reshape(1, -1), wa, wb, wk, wv,
                            batch, seq)
        att = _attention(q, k, v, batch, seq)
        ret = _retention(proj, rc, rs, batch, seq)
        wrh, wrl, br = _prep_router(w_route_group[l], b_route_group[l], w_route_expert[l], b_route_expert[l])
        cw = jnp.pad(conv_w[l], ((0, SUBLANES - CONV_K), (0, 0)))
        x2, h, rout, cnt = _mix(proj, att, ret, x2, mod3, norm_ffn_g[l].reshape(1, d), cw,
                           w_o_mla[l].astype(BF16), w_o_conv[l].astype(BF16), w_o_ret[l].astype(BF16),
                           w_mix_out[l].astype(BF16), wrh, wrl, br, seq)
        moe_out = _moe(h, rout, cnt, w_exp_gate, w_exp_up, w_exp_down, l)
        x2 = _combine(x2, moe_out, mod3, final_g.reshape(1, d), seq, final=(l == depth - 1))
    return x2.reshape(batch, seq, d)
```

```python
import functools

import jax
import jax.numpy as jnp
from jax import lax
from jax.experimental import pallas as pl
from jax.experimental.pallas import tpu as pltpu

F32 = jnp.float32
BF16 = jnp.bfloat16

D_MODEL = 1024
CHUNK = 64
EPS = 1e-6
ROPE_THETA = 10000.0
LOG2E = 1.4426950408889634

MLA_HEADS = 8
MLA_Q_RANK = 384
MLA_KV_RANK = 256
MLA_NOPE = 64
MLA_ROPE = 32
MLA_V = 64

CONV_WIDTH = 512
CONV_K = 3

RET_HEADS = 4
RET_DK = 128
RET_DV = 256

N_GROUPS = 4
EXPERTS_PER_GROUP = 8
N_EXPERTS = N_GROUPS * EXPERTS_PER_GROUP
EXPERT_FF = 512

LANES = 128
SUBLANES = 8
VMEM_LIMIT = 52 * 1024 * 1024

GATE_W = 3 * D_MODEL
RQ_OFF = GATE_W
RK_OFF = RQ_OFF + RET_HEADS * RET_DK
RV_OFF = RK_OFF + RET_HEADS * RET_DK
RG_OFF = RV_OFF + RET_HEADS * RET_DV
CONV_OFF = RG_OFF + RET_HEADS * RET_DV
MLA_OFF = CONV_OFF + 3 * CONV_WIDTH
MLA_W = MLA_Q_RANK + MLA_KV_RANK + LANES
PROJ_W = MLA_OFF + MLA_W
PROJ_N_TILES = 3
PROJ_TN = PROJ_W // PROJ_N_TILES
assert PROJ_TN * PROJ_N_TILES == PROJ_W and PROJ_TN % LANES == 0
assert MLA_OFF % MLA_W == 0 and CONV_OFF % (3 * CONV_WIDTH) == 0

QPAD = LANES
ROUTE_E_OFF = 32

ROW_TILE_PROJ = 1024
ROW_TILE = 512
ATT_TILE = 512
ATT_SUB = 256
ATT_GROUP = 4
LAZY_MAX_HEADROOM = 60.0
RET_BLOCK = 256
MOE_TILE = 256
ADA_TN = 768


def _params(sem, vmem=VMEM_LIMIT):
    return pltpu.CompilerParams(dimension_semantics=sem, vmem_limit_bytes=vmem)


def _const_spec(shape):
    nd = len(shape)
    return pl.BlockSpec(shape, lambda *_: (0,) * nd)


ROW_SPLIT = D_MODEL // LANES
assert ROW_SPLIT == SUBLANES


def _store_token_tiles(ref, base, rows, val):
    for s in range(ROW_SPLIT):
        ref[pl.ds(base + s, rows, stride=ROW_SPLIT), :] = val[:, s * LANES:(s + 1) * LANES]


def _load_token_tiles(ref, base, rows):
    return jnp.concatenate([ref[pl.ds(base + s, rows, stride=ROW_SPLIT), :] for s in range(ROW_SPLIT)], axis=1)


def _rms(x, g):
    return x * lax.rsqrt(jnp.mean(x * x, axis=-1, keepdims=True) + EPS) * g


def _sigmoid(x):
    return 1.0 / (1.0 + jnp.exp(-x))


def _ada_kernel(ct_ref, w_ref, b_ref, o_ref):
    ct = ct_ref[...]
    a = ct * _sigmoid(ct)
    w = w_ref[0]
    rows = [jnp.sum(w * a[:, b:b + 1], axis=0, keepdims=True) for b in range(ct.shape[1])]
    o_ref[0] = jnp.concatenate(rows, axis=0) + b_ref[0]


def _ada_mod(c, w_ada, b_ada):
    depth, d, n = w_ada.shape
    b = c.shape[0]
    return pl.pallas_call(
        _ada_kernel,
        grid=(depth, n // ADA_TN),
        in_specs=[
            _const_spec((d, b)),
            pl.BlockSpec((1, d, ADA_TN), lambda l, j: (l, 0, j)),
            pl.BlockSpec((1, 1, ADA_TN), lambda l, j: (l, 0, j)),
        ],
        out_specs=pl.BlockSpec((1, b, ADA_TN), lambda l, j: (l, 0, j)),
        out_shape=jax.ShapeDtypeStruct((depth, b, n), F32),
        compiler_params=_params(("arbitrary", "arbitrary")),
        name="ada_mod",
    )(c.T, w_ada, b_ada.reshape(depth, 1, n))


def _rope_kernel(pos_ref, f_ref, rc_ref, rs_ref, mc_ref, ms_ref):
    ang = pos_ref[...].astype(F32) * f_ref[...]
    cs = jnp.cos(ang)
    sn = jnp.sin(ang)
    lane = lax.broadcasted_iota(jnp.int32, ang.shape, 1)
    half = LANES // 2
    lo = lane < half
    rc_ref[...] = jnp.where(lo, cs, pltpu.roll(cs, half, 1))
    rs_ref[...] = jnp.where(lo, -sn, pltpu.roll(sn, half, 1))
    rope = (lane >= MLA_NOPE) & (lane < MLA_NOPE + MLA_ROPE)
    mc_ref[...] = jnp.where(lane < MLA_NOPE, 1.0, jnp.where(rope, cs, 0.0))
    ms_ref[...] = jnp.where(rope, sn, 0.0)


def _rope_tables(positions):
    t = positions.size
    tm = min(2048, t)
    inv_ret = ROPE_THETA ** (-jnp.arange(0, RET_DK, 2, dtype=F32) / RET_DK)
    inv_mla = ROPE_THETA ** (-jnp.arange(0, MLA_ROPE, 2, dtype=F32) / MLA_ROPE)
    freqs = jnp.concatenate([inv_ret, inv_mla, inv_mla, jnp.zeros((LANES - 96,), F32)]).reshape(1, LANES)
    spec = pl.BlockSpec((tm, LANES), lambda i: (i, 0))
    return pl.pallas_call(
        _rope_kernel,
        grid=(t // tm,),
        in_specs=[pl.BlockSpec((tm, 1), lambda i: (i, 0)), _const_spec((1, LANES))],
        out_specs=[spec] * 4,
        out_shape=[jax.ShapeDtypeStruct((t, LANES), F32)] * 4,
        compiler_params=_params(("arbitrary",)),
        name="rope_tables",
    )(positions.reshape(t, 1), freqs)


def _proj_kernel(x_ref, sh_ref, sc_ref, g_ref, w_ref, o_ref, h_scr):
    @pl.when(pl.program_id(1) == 0)
    def _():
        h = _rms(x_ref[...], g_ref[...]) * (1.0 + sc_ref[0]) + sh_ref[0]
        h_scr[...] = h.astype(BF16)

    o_ref[...] = jnp.dot(h_scr[...], w_ref[0], preferred_element_type=F32).astype(BF16)


def _in_proj(x2, mod3, g, w_all, layer, seq):
    t, d = x2.shape
    tm = min(ROW_TILE_PROJ, seq)
    per_b = seq // tm
    return pl.pallas_call(
        _proj_kernel,
        grid=(t // tm, PROJ_W // PROJ_TN),
        in_specs=[
            pl.BlockSpec((tm, d), lambda i, j: (i, 0)),
            pl.BlockSpec((1, 1, d), lambda i, j: ((i // per_b) * 6 + 0, 0, 0)),
            pl.BlockSpec((1, 1, d), lambda i, j: ((i // per_b) * 6 + 1, 0, 0)),
            _const_spec((1, d)),
            pl.BlockSpec((1, d, PROJ_TN), lambda i, j: (layer, 0, j)),
        ],
        out_specs=pl.BlockSpec((tm, PROJ_TN), lambda i, j: (i, j)),
        out_shape=jax.ShapeDtypeStruct((t, PROJ_W), BF16),
        scratch_shapes=[pltpu.VMEM((tm, d), BF16)],
        compiler_params=_params(("arbitrary", "arbitrary")),
        name="in_proj",
    )(x2, mod3, mod3, g, w_all)


def _mla_prep_kernel(in_ref, mc_ref, ms_ref, gq_ref, gkv_ref, wa_ref, wb_ref, wk_ref, wv_ref, q_ref, k_ref, v_ref):
    blk = in_ref[...].astype(F32)
    qn = _rms(blk[:, :MLA_Q_RANK], gq_ref[...]).astype(BF16)
    kvn = _rms(blk[:, MLA_Q_RANK:MLA_Q_RANK + MLA_KV_RANK], gkv_ref[...]).astype(BF16)
    kb = blk[:, MLA_Q_RANK + MLA_KV_RANK:]
    mc = mc_ref[...]
    ms = ms_ref[...]
    qa = jnp.dot(qn, wa_ref[...], preferred_element_type=F32)
    kn = jnp.dot(kvn, wk_ref[...], preferred_element_type=F32)
    lane = lax.broadcasted_iota(jnp.int32, kb.shape, 1)
    rope = (lane >= MLA_NOPE) & (lane < MLA_NOPE + MLA_ROPE)
    kpe = jnp.where(rope, kb * mc + pltpu.roll(kb, LANES // 2, 1) * ms, 0.0)
    scale = (MLA_NOPE + MLA_ROPE) ** -0.5 * LOG2E
    for h in range(MLA_HEADS):
        sl = slice(h * QPAD, (h + 1) * QPAD)
        qh = qa[:, sl]
        half = MLA_ROPE // 2
        partner = jnp.where(lane < MLA_NOPE + half, -pltpu.roll(qh, LANES - half, 1), pltpu.roll(qh, half, 1))
        q_ref[:, sl] = ((qh * mc + partner * ms) * scale).astype(BF16)
        k_ref[:, sl] = (kn[:, sl] + kpe).astype(BF16)
    vt = lax.dot_general(wv_ref[...], kvn, (((1,), (1,)), ((), ())), preferred_element_type=F32)
    row = lax.broadcasted_iota(jnp.int32, vt.shape, 0)
    v_ref[0, 0] = jnp.where(row % QPAD >= MLA_V, 1.0, vt).astype(BF16)


def _mla_prep(proj, mc, ms, gq, gkv, wa, wb, wk, wv, batch, seq):
    t = proj.shape[0]
    tm = min(ATT_TILE, seq)
    per_b = seq // tm
    hq = MLA_HEADS * QPAD
    hv = MLA_HEADS * QPAD
    return pl.pallas_call(
        _mla_prep_kernel,
        grid=(t // tm,),
        in_specs=[
            pl.BlockSpec((tm, MLA_W), lambda i: (i, MLA_OFF // MLA_W)),
            pl.BlockSpec((tm, LANES), lambda i: (i, 0)),
            pl.BlockSpec((tm, LANES), lambda i: (i, 0)),
            _const_spec((1, MLA_Q_RANK)),
            _const_spec((1, MLA_KV_RANK)),
            _const_spec((MLA_Q_RANK, hq)),
            _const_spec((MLA_Q_RANK, hq)),
            _const_spec((MLA_KV_RANK, hq)),
            _const_spec((hv, MLA_KV_RANK)),
        ],
        out_specs=[
            pl.BlockSpec((tm, hq), lambda i: (i, 0)),
            pl.BlockSpec((tm, hq), lambda i: (i, 0)),
            pl.BlockSpec((1, 1, hv, tm), lambda i: (i // per_b, i % per_b, 0, 0)),
        ],
        out_shape=[
            jax.ShapeDtypeStruct((t, hq), BF16),
            jax.ShapeDtypeStruct((t, hq), BF16),
            jax.ShapeDtypeStruct((batch, per_b, hv, tm), BF16),
        ],
        compiler_params=_params(("arbitrary",)),
        name="mla_prep",
    )(proj, mc, ms, gq, gkv, wa, wb, wk, wv)


def _attn_kernel(q_ref, k_ref, vt_ref, o_ref, m_scr, acc_scr, *, tile):
    qi = pl.program_id(2)
    nn_dims = (((1,), (1,)), ((), ()))

    def head_q(a):
        return q_ref[0, :, a * QPAD:(a + 1) * QPAD]

    def head_k(start, size, a):
        return k_ref[0, pl.ds(start, size), a * QPAD:(a + 1) * QPAD]

    def exact_step(j, src, dst, first, masked):
        start = pl.multiple_of(j * tile, tile)
        for a in range(2):
            st = lax.dot_general(head_k(start, tile, a), head_q(a), nn_dims, preferred_element_type=F32)
            if masked:
                key = lax.broadcasted_iota(jnp.int32, (tile, tile), 0)
                qry = lax.broadcasted_iota(jnp.int32, (tile, tile), 1)
                st = jnp.where((key // CHUNK) <= (qry // CHUNK), st, -jnp.inf)
            m_new = jnp.max(st, axis=0, keepdims=True)
            if not first:
                m_old = m_scr[src, a]
                m_new = jnp.maximum(m_old, m_new)
            p = jnp.exp2(st - m_new).astype(BF16)
            vt = vt_ref[0, j, a * QPAD:(a + 1) * QPAD, :]
            pv = jnp.dot(vt, p, preferred_element_type=F32)
            if not first:
                pv = pv + jnp.exp2(m_old - m_new) * acc_scr[src, a]
            acc_scr[dst, a] = pv
            m_scr[dst, a] = m_new

    exact_step(qi, 0, 0, first=True, masked=True)

    def lazy_step(j0, n_tiles, cur):
        nxt = 1 - cur
        n_sub = tile // ATT_SUB
        scores = {}
        for a in range(2):
            for t in range(n_tiles):
                start = pl.multiple_of((j0 + t) * tile, tile)
                for s in range(n_sub):
                    k = head_k(start + s * ATT_SUB, ATT_SUB, a)
                    scores[a, t, s] = lax.dot_general(k, head_q(a), nn_dims, preferred_element_type=F32)
        excess = jnp.zeros((1, tile), F32)
        for a in range(2):
            m_used = m_scr[cur, a]
            m_new = m_used
            pv = jnp.zeros((QPAD, tile), F32)
            for t in range(n_tiles):
                for s in range(n_sub):
                    st = scores[a, t, s]
                    m_new = jnp.maximum(m_new, jnp.max(st, axis=0, keepdims=True))
                    p = jnp.exp2(st - m_used).astype(BF16)
                    vt = vt_ref[0, j0 + t, a * QPAD:(a + 1) * QPAD, s * ATT_SUB:(s + 1) * ATT_SUB]
                    pv = pv + jnp.dot(vt, p, preferred_element_type=F32)
            acc_scr[nxt, a] = (acc_scr[cur, a] + pv) * jnp.exp2(m_used - m_new)
            m_scr[nxt, a] = m_new
            excess = jnp.maximum(excess, m_new - m_used)

        @pl.when(jnp.max(excess) > LAZY_MAX_HEADROOM)
        def _():
            def redo(t, carry):
                exact_step(j0 + t, jnp.where(t == 0, cur, nxt), nxt, first=False, masked=False)
                return carry

            lax.fori_loop(0, n_tiles, redo, 0)

        return nxt

    n_groups = qi // ATT_GROUP
    cur = lax.fori_loop(0, n_groups, lambda g, c: lazy_step(g * ATT_GROUP, ATT_GROUP, c), 0)
    done = n_groups * ATT_GROUP
    size = ATT_GROUP // 2
    while size >= 1:
        take = (qi - done) >= size

        @pl.when(take)
        def _(done=done, size=size, cur=cur):
            lazy_step(done, size, cur)

        cur = jnp.where(take, 1 - cur, cur)
        done = jnp.where(take, done + size, done)
        size //= 2

    lane = lax.broadcasted_iota(jnp.int32, (tile, LANES), 1)
    r0 = acc_scr[cur, 0].T
    r1 = acc_scr[cur, 1].T
    o0 = r0 / r0[:, MLA_V:MLA_V + 1]
    o1 = pltpu.roll(r1, MLA_V, 1) / r1[:, MLA_V:MLA_V + 1]
    o_ref[0] = jnp.where(lane < MLA_V, o0, o1).astype(BF16)


def _attention(q, k, vt, batch, seq):
    tile = min(ATT_TILE, seq)
    q3 = q.reshape(batch, seq, MLA_HEADS * QPAD)
    k3 = k.reshape(batch, seq, MLA_HEADS * QPAD)
    out = pl.pallas_call(
        functools.partial(_attn_kernel, tile=tile),
        grid=(batch, MLA_HEADS // 2, seq // tile),
        in_specs=[
            pl.BlockSpec((1, tile, 2 * QPAD), lambda b, h, i: (b, i, h)),
            pl.BlockSpec((1, seq, 2 * QPAD), lambda b, h, i: (b, 0, h)),
            pl.BlockSpec((1, seq // tile, 2 * QPAD, tile), lambda b, h, i: (b, 0, h, 0)),
        ],
        out_specs=pl.BlockSpec((1, tile, 2 * MLA_V), lambda b, h, i: (b, i, h)),
        out_shape=jax.ShapeDtypeStruct((batch, seq, MLA_HEADS * MLA_V), BF16),
        scratch_shapes=[
            pltpu.VMEM((2, 2, 1, tile), F32),
            pltpu.VMEM((2, 2, QPAD, tile), F32),
        ],
        compiler_params=_params(("arbitrary", "arbitrary", "arbitrary")),
        name="mla_attention",
    )(q3, k3, vt)
    return out.reshape(batch * seq, MLA_HEADS * MLA_V)


def _ret_kernel(lg_ref, q_ref, k_ref, v_ref, g_ref, rc_ref, rs_ref, o_ref, state_scr, decay_scr, *, blk):
    n = pl.program_id(1)

    @pl.when(n == 0)
    def _():
        state_scr[...] = jnp.zeros(state_scr.shape, F32)
        row = lax.broadcasted_iota(jnp.int32, (blk, blk), 0)
        col = lax.broadcasted_iota(jnp.int32, (blk, blk), 1)
        dist = jnp.abs(row - col).astype(F32)
        visible = (col // CHUNK) <= (row // CHUNK)
        for h in range(RET_HEADS):
            decay_scr[h] = jnp.where(visible, jnp.exp(lg_ref[h] * dist), 0.0)

    rc = rc_ref[...]
    rs = rs_ref[...]
    half = RET_DK // 2
    pos = lax.broadcasted_iota(jnp.int32, (blk, 1), 0).astype(F32)
    for h in range(RET_HEADS):
        lg = lg_ref[h]
        ksl = slice(h * RET_DK, (h + 1) * RET_DK)
        vsl = slice(h * RET_DV, (h + 1) * RET_DV)
        qf = q_ref[:, ksl].astype(F32)
        kf = k_ref[:, ksl].astype(F32)
        q = qf * rc + pltpu.roll(qf, half, 1) * rs
        k = (kf * rc + pltpu.roll(kf, half, 1) * rs) * (RET_DK ** -0.5)
        v = v_ref[:, vsl]
        q_dec = jnp.exp(lg * (pos + 1.0))
        k_dec = jnp.exp(lg * (blk - 1.0 - pos))

        s = lax.dot_general(q.astype(BF16), k.astype(BF16), (((1,), (1,)), ((), ())), preferred_element_type=F32)
        s = s * decay_scr[h]
        state = state_scr[h]
        o = jnp.dot(s.astype(BF16), v, preferred_element_type=F32)
        o = o + jnp.dot((q * q_dec).astype(BF16), state.astype(BF16), preferred_element_type=F32)
        kd = (k * k_dec).astype(BF16)
        upd = lax.dot_general(kd, v, (((0,), (0,)), ((), ())), preferred_element_type=F32)
        state_scr[h] = state * jnp.exp(lg * blk) + upd

        o = o * lax.rsqrt(jnp.mean(o * o, axis=-1, keepdims=True) + EPS)
        g = g_ref[:, vsl].astype(F32)
        o_ref[:, vsl] = (g * _sigmoid(g) * o).astype(BF16)


def _retention(proj, rc, rs, batch, seq):
    t = proj.shape[0]
    blk = min(RET_BLOCK, seq)
    per_b = seq // blk
    log_gamma = jnp.log(1.0 - 2.0 ** (-5.0 - jnp.arange(RET_HEADS, dtype=F32)))
    qk_w = RET_HEADS * RET_DK
    v_w = RET_HEADS * RET_DV
    row = lambda b, n: b * per_b + n
    return pl.pallas_call(
        functools.partial(_ret_kernel, blk=blk),
        grid=(batch, per_b),
        in_specs=[
            pl.BlockSpec(memory_space=pltpu.SMEM),
            pl.BlockSpec((blk, qk_w), lambda b, n: (row(b, n), RQ_OFF // qk_w)),
            pl.BlockSpec((blk, qk_w), lambda b, n: (row(b, n), RK_OFF // qk_w)),
            pl.BlockSpec((blk, v_w), lambda b, n: (row(b, n), RV_OFF // v_w)),
            pl.BlockSpec((blk, v_w), lambda b, n: (row(b, n), RG_OFF // v_w)),
            pl.BlockSpec((blk, LANES), lambda b, n: (row(b, n), 0)),
            pl.BlockSpec((blk, LANES), lambda b, n: (row(b, n), 0)),
        ],
        out_specs=pl.BlockSpec((blk, v_w), lambda b, n: (row(b, n), 0)),
        out_shape=jax.ShapeDtypeStruct((t, v_w), BF16),
        scratch_shapes=[pltpu.VMEM((RET_HEADS, RET_DK, RET_DV), F32), pltpu.VMEM((RET_HEADS, blk, blk), F32)],
        compiler_params=_params(("arbitrary", "arbitrary")),
        name="retention",
    )(log_gamma, proj, proj, proj, proj, rc, rs)


def _mix_kernel(gate_ref, conv_ref, att_ref, ret_ref, x_ref, gt_ref, sh_ref, sc_ref, gf_ref, cw_ref,
                wom_ref, woc_ref, wor_ref, wmo_ref, wrh_ref, wrl_ref, br_ref,
                xo_ref, h_ref, rout_ref, cnt_ref, u_scr, *, tm, per_b):
    i = pl.program_id(0)
    d = D_MODEL
    cw = CONV_WIDTH

    cb = conv_ref[:, 0:cw].astype(F32)
    u = conv_ref[:, cw:2 * cw].astype(F32) * conv_ref[:, 2 * cw:3 * cw].astype(F32)

    @pl.when(i % per_b == 0)
    def _():
        u_scr[0:SUBLANES, :] = jnp.zeros((SUBLANES, cw), F32)

    @pl.when(i % per_b != 0)
    def _():
        u_scr[0:SUBLANES, :] = u_scr[tm:tm + SUBLANES, :]

    u_scr[SUBLANES:SUBLANES + tm, :] = u
    y = cw_ref[2:3, :] * u
    y = y + cw_ref[1:2, :] * u_scr[SUBLANES - 1:SUBLANES - 1 + tm, :]
    y = y + cw_ref[0:1, :] * u_scr[SUBLANES - 2:SUBLANES - 2 + tm, :]
    z = (cb * y).astype(BF16)

    y_mla = jnp.dot(att_ref[...], wom_ref[...], preferred_element_type=F32)
    y_conv = jnp.dot(z, woc_ref[...], preferred_element_type=F32)
    y_ret = jnp.dot(ret_ref[...], wor_ref[...], preferred_element_type=F32)
    merged = _sigmoid(gate_ref[:, 0:d].astype(F32)) * y_mla
    merged = merged + _sigmoid(gate_ref[:, d:2 * d].astype(F32)) * y_conv
    merged = merged + _sigmoid(gate_ref[:, 2 * d:3 * d].astype(F32)) * y_ret
    x = x_ref[...] + gt_ref[0] * jnp.dot(merged.astype(BF16), wmo_ref[...], preferred_element_type=F32)
    xo_ref[...] = x

    h = _rms(x, gf_ref[...]) * (1.0 + sc_ref[0]) + sh_ref[0]
    _store_token_tiles(h_ref, 0, tm, h)

    h_hi = h.astype(BF16)
    h_lo = (h - h_hi.astype(F32)).astype(BF16)
    lg = jnp.dot(h_hi, wrh_ref[...], preferred_element_type=F32)
    lg = lg + jnp.dot(h_lo, wrh_ref[...], preferred_element_type=F32)
    lg = lg + jnp.dot(h_hi, wrl_ref[...], preferred_element_type=F32)
    lg = lg + br_ref[...]

    lane = lax.broadcasted_iota(jnp.int32, lg.shape, 1)
    neg = -jnp.inf
    gl = jnp.where(lane < N_GROUPS, lg, neg)
    gmax = jnp.max(gl, axis=-1, keepdims=True)
    g_val = 1.0 / jnp.sum(jnp.exp(gl - gmax), axis=-1, keepdims=True)
    g_idx = jnp.min(jnp.where(gl == gmax, lane, LANES), axis=-1, keepdims=True)
    in_group = (lane >= ROUTE_E_OFF) & (lane < ROUTE_E_OFF + N_EXPERTS)
    in_group = in_group & (((lane - ROUTE_E_OFF) // EXPERTS_PER_GROUP) == g_idx)
    el = jnp.where(in_group, lg, neg)
    m1 = jnp.max(el, axis=-1, keepdims=True)
    i1 = jnp.min(jnp.where(el == m1, lane, LANES), axis=-1, keepdims=True)
    el2 = jnp.where(lane == i1, neg, el)
    m2 = jnp.max(el2, axis=-1, keepdims=True)
    i2 = jnp.min(jnp.where(el2 == m2, lane, LANES), axis=-1, keepdims=True)
    r = jnp.exp(m2 - m1)
    w1 = g_val / (1.0 + r)
    w2 = g_val * r / (1.0 + r)
    e1 = (i1 - ROUTE_E_OFF).astype(F32)
    e2 = (i2 - ROUTE_E_OFF).astype(F32)
    rout = jnp.where(lane == 0, e1, jnp.where(lane == 1, e2, jnp.where(lane == 2, w1, jnp.where(lane == 3, w2, 0.0))))
    rout_ref[...] = rout.T[0:SUBLANES, :]

    hits = jnp.where(lane == i1, 1.0, 0.0) + jnp.where(lane == i2, 1.0, 0.0)

    @pl.when(i == 0)
    def _():
        cnt_ref[...] = jnp.zeros(cnt_ref.shape, F32)

    cnt_ref[0:1, :] = cnt_ref[0:1, :] + jnp.sum(hits, axis=0, keepdims=True)


def _mix(proj, att, ret, x2, mod3, gf, cw, wom, woc, wor, wmo, wrh, wrl, br, seq):
    t, d = x2.shape
    tm = min(ROW_TILE, seq)
    per_b = seq // tm
    row = lambda i: (i, 0)
    modspec = lambda k: pl.BlockSpec((1, 1, d), lambda i: ((i // per_b) * 6 + k, 0, 0))
    return pl.pallas_call(
        functools.partial(_mix_kernel, tm=tm, per_b=per_b),
        grid=(t // tm,),
        in_specs=[
            pl.BlockSpec((tm, GATE_W), row),
            pl.BlockSpec((tm, 3 * CONV_WIDTH), lambda i: (i, CONV_OFF // (3 * CONV_WIDTH))),
            pl.BlockSpec((tm, MLA_HEADS * MLA_V), row),
            pl.BlockSpec((tm, RET_HEADS * RET_DV), row),
            pl.BlockSpec((tm, d), row),
            modspec(2), modspec(3), modspec(4),
            _const_spec((1, d)),
            _const_spec((SUBLANES, CONV_WIDTH)),
            _const_spec(wom.shape), _const_spec(woc.shape), _const_spec(wor.shape), _const_spec(wmo.shape),
            _const_spec(wrh.shape), _const_spec(wrl.shape), _const_spec((1, LANES)),
        ],
        out_specs=[pl.BlockSpec((tm, d), row), pl.BlockSpec((tm * ROW_SPLIT, LANES), row),
                   pl.BlockSpec((SUBLANES, tm), lambda i: (0, i)), _const_spec((SUBLANES, LANES))],
        out_shape=[
            jax.ShapeDtypeStruct((t, d), F32),
            jax.ShapeDtypeStruct((t * ROW_SPLIT, LANES), F32),
            jax.ShapeDtypeStruct((SUBLANES, t), F32),
            jax.ShapeDtypeStruct((SUBLANES, LANES), F32),
        ],
        scratch_shapes=[pltpu.VMEM((tm + SUBLANES, CONV_WIDTH), F32)],
        compiler_params=_params(("arbitrary",)),
        name="mix_router",
    )(proj, proj, att, ret, x2, mod3, mod3, mod3, gf, cw, wom, woc, wor, wmo, wrh, wrl, br)


def _moe_kernel(te_ref, nv_ref, meta_ref, meta_next_ref, wt_ref, wg_ref, wu_ref, wd_ref, h_hbm, out_hbm,
                xbuf, ybuf, wgb, wub, wdb, gsem, ssem, *, tm):
    i = pl.program_id(0)
    nt = pl.num_programs(0)
    slot = i % 2
    nv = nv_ref[i]
    prev = jnp.maximum(i - 1, 0)
    prev2 = jnp.maximum(i - 2, 0)

    def gather_copy(tok, s, r):
        src = h_hbm.at[pl.ds(pl.multiple_of(tok * ROW_SPLIT, ROW_SPLIT), ROW_SPLIT)]
        return pltpu.make_async_copy(src, xbuf.at[pl.ds((s * tm + r) * ROW_SPLIT, ROW_SPLIT)], gsem.at[s])

    def scatter_copy(dst, s, r):
        dst_rows = out_hbm.at[pl.ds(pl.multiple_of(dst * ROW_SPLIT, ROW_SPLIT), ROW_SPLIT)]
        return pltpu.make_async_copy(ybuf.at[pl.ds((s * tm + r) * ROW_SPLIT, ROW_SPLIT)], dst_rows, ssem.at[s])

    def start_gather(meta, s):
        for r in range(tm):
            gather_copy(meta[0, 0, r], s, r).start(priority=r % 2)

    def wait_gather(s):
        for r in range(tm):
            gather_copy(0, s, r).wait()

    def start_scatter(s):
        for r in range(tm):
            scatter_copy(meta_ref[0, 0, tm + r], s, r).start(priority=r % 2)

    def wait_scatter(s):
        for r in range(tm):
            scatter_copy(0, s, r).wait()

    def for_slot(fn):
        @pl.when(slot == 0)
        def _():
            fn(0)

        @pl.when(slot == 1)
        def _():
            fn(1)

    @pl.when(i == 0)
    def _():
        ybuf[...] = jnp.zeros(ybuf.shape, F32)
        half = tm * ROW_SPLIT
        for s in range(2):
            pad_rows = out_hbm.at[pl.ds(out_hbm.shape[0] - (2 - s) * half, half)]
            init = pltpu.make_async_copy(ybuf.at[pl.ds(s * half, half)], pad_rows, ssem.at[s])
            init.start()
            init.wait()
        start_gather(meta_ref, 0)

    @pl.when((i >= 2) & (nv_ref[prev2] > 0))
    def _():
        for_slot(wait_scatter)

    @pl.when((i == 0) | (te_ref[i] != te_ref[prev]))
    def _():
        wgb[...] = wg_ref[0].astype(BF16)
        wub[...] = wu_ref[0].astype(BF16)
        wdb[...] = wd_ref[0].astype(BF16)

    def active(s):
        wait_gather(s)
        x = _load_token_tiles(xbuf, s * tm * ROW_SPLIT, tm).astype(BF16)
        start_gather(meta_next_ref, 1 - s)
        g = jnp.dot(x, wgb[...], preferred_element_type=F32)
        u = jnp.dot(x, wub[...], preferred_element_type=F32)
        hid = (g * _sigmoid(g) * u).astype(BF16)
        w_col = jnp.broadcast_to(wt_ref[0], (LANES, tm)).T[:, 0:1]
        y = jnp.dot(hid, wdb[...], preferred_element_type=F32) * w_col
        _store_token_tiles(ybuf, s * tm * ROW_SPLIT, tm, y)
        start_scatter(s)

    @pl.when(nv > 0)
    def _():
        for_slot(active)

    @pl.when((i > 0) & (nv == 0) & (nv_ref[prev] > 0))
    def _():
        for_slot(wait_gather)

    @pl.when((i == nt - 1) & (nv_ref[prev] > 0))
    def _():
        for_slot(lambda s: wait_scatter(1 - s))


def _route_tables(rout, cnt, t, tm, nt):
    e = rout[0:2].astype(jnp.int32).reshape(-1)
    w = rout[2:4].reshape(-1)
    n_slot = 2 * t
    perm = jnp.argsort(e).astype(jnp.int32)
    counts = cnt[0, ROUTE_E_OFF:ROUTE_E_OFF + N_EXPERTS].astype(jnp.int32)
    nt_e = (counts + tm - 1) // tm
    tile_end = jnp.cumsum(nt_e)
    tile_start = tile_end - nt_e
    off = jnp.cumsum(counts) - counts
    total = tile_end[-1]
    i = jnp.arange(nt, dtype=jnp.int32)
    i_act = jnp.minimum(i, total - 1)
    e_i = jnp.minimum(jnp.sum((tile_end[None, :] <= i_act[:, None]).astype(jnp.int32), axis=1), N_EXPERTS - 1)
    sel = e_i[:, None] == jnp.arange(N_EXPERTS, dtype=jnp.int32)[None, :]
    pick = lambda v: jnp.sum(jnp.where(sel, v[None, :], 0), axis=1)
    j = i - pick(tile_start)
    nvalid = jnp.where(i < total, jnp.clip(pick(counts) - j * tm, 0, tm), 0).astype(jnp.int32)
    r = jnp.arange(tm, dtype=jnp.int32)
    pos = jnp.clip((pick(off) + j * tm)[:, None] + r[None, :], 0, n_slot - 1)
    slot = perm[pos]
    valid = r[None, :] < nvalid[:, None]
    tok = jnp.where(slot >= t, slot - t, slot)
    dst = jnp.where(valid, slot, n_slot + (i % 2)[:, None] * tm + r[None, :])
    wt = jnp.where(valid, w[slot], 0.0)
    meta = jnp.concatenate([tok, dst], axis=1).astype(jnp.int32).reshape(nt, 1, 2 * tm)
    return e_i, nvalid, meta, wt.reshape(nt, 1, tm)


def _moe(h, rout, cnt, wg, wu, wd, layer):
    t, d = h.shape[0] // ROW_SPLIT, D_MODEL
    tm = min(MOE_TILE, t)
    nt = (2 * t) // tm + N_EXPERTS
    e_i, nvalid, meta, wt = _route_tables(rout, cnt, t, tm, nt)
    f = wg.shape[-1]
    grid_spec = pltpu.PrefetchScalarGridSpec(
        num_scalar_prefetch=2,
        grid=(nt,),
        in_specs=[
            pl.BlockSpec((1, 1, 2 * tm), lambda i, te, nv: (i, 0, 0), memory_space=pltpu.SMEM),
            pl.BlockSpec((1, 1, 2 * tm), lambda i, te, nv: (jnp.minimum(i + 1, nt - 1), 0, 0), memory_space=pltpu.SMEM),
            pl.BlockSpec((1, 1, tm), lambda i, te, nv: (i, 0, 0)),
            pl.BlockSpec((1, 1, d, f), lambda i, te, nv: (layer, te[i], 0, 0)),
            pl.BlockSpec((1, 1, d, f), lambda i, te, nv: (layer, te[i], 0, 0)),
            pl.BlockSpec((1, 1, f, d), lambda i, te, nv: (layer, te[i], 0, 0)),
            pl.BlockSpec(memory_space=pl.ANY),
        ],
        out_specs=pl.BlockSpec(memory_space=pl.ANY),
        scratch_shapes=[
            pltpu.VMEM((2 * tm * ROW_SPLIT, LANES), F32),
            pltpu.VMEM((2 * tm * ROW_SPLIT, LANES), F32),
            pltpu.VMEM((d, f), BF16),
            pltpu.VMEM((d, f), BF16),
            pltpu.VMEM((f, d), BF16),
            pltpu.SemaphoreType.DMA((2,)),
            pltpu.SemaphoreType.DMA((2,)),
        ],
    )

    def kern(te_ref, nv_ref, meta_ref, meta_next_ref, wt_ref, wg_ref, wu_ref, wd_ref, h_hbm, out_hbm, *scratch):
        _moe_kernel(te_ref, nv_ref, meta_ref, meta_next_ref, wt_ref, wg_ref.at[0], wu_ref.at[0], wd_ref.at[0],
                    h_hbm, out_hbm, *scratch, tm=tm)

    return pl.pallas_call(
        kern,
        grid_spec=grid_spec,
        out_shape=jax.ShapeDtypeStruct(((2 * t + 2 * tm) * ROW_SPLIT, LANES), F32),
        compiler_params=_params(("arbitrary",)),
        name="moe_experts",
    )(e_i, nvalid, meta, meta, wt, wg, wu, wd, h)


def _combine_kernel(x_ref, a_ref, b_ref, gt_ref, g_ref, o_ref, *, final):
    tm = x_ref.shape[0]
    moe = _load_token_tiles(a_ref, 0, tm) + _load_token_tiles(b_ref, 0, tm)
    x = x_ref[...] + gt_ref[0] * moe
    if final:
        x = _rms(x, g_ref[...])
    o_ref[...] = x


def _combine(x2, moe_out, mod3, final_g, seq, final):
    t, d = x2.shape
    tm = min(512, seq)
    per_b = seq // tm
    return pl.pallas_call(
        functools.partial(_combine_kernel, final=final),
        grid=(t // tm,),
        in_specs=[
            pl.BlockSpec((tm, d), lambda i: (i, 0)),
            pl.BlockSpec((tm * ROW_SPLIT, LANES), lambda i: (i, 0)),
            pl.BlockSpec((tm * ROW_SPLIT, LANES), lambda i: (t // tm + i, 0)),
            pl.BlockSpec((1, 1, d), lambda i: ((i // per_b) * 6 + 5, 0, 0)),
            _const_spec((1, d)),
        ],
        out_specs=pl.BlockSpec((tm, d), lambda i: (i, 0)),
        out_shape=jax.ShapeDtypeStruct((t, d), F32),
        compiler_params=_params(("arbitrary",)),
        name="combine",
    )(x2, moe_out, moe_out, mod3, final_g)


def _prep_w_in(w):
    sizes = (MLA_Q_RANK, MLA_KV_RANK, MLA_ROPE, CONV_WIDTH, CONV_WIDTH, CONV_WIDTH,
             RET_HEADS * RET_DK, RET_HEADS * RET_DK, RET_HEADS * RET_DV, RET_HEADS * RET_DV, 3 * D_MODEL)
    parts = []
    start = 0
    for size in sizes:
        parts.append(w[..., start:start + size])
        start += size
    q_lat, kv_lat, kr, cb, cc, cx, rq, rk, rv, rg, gl = parts
    half = MLA_ROPE // 2
    zeros = lambda n: jnp.zeros(w.shape[:-1] + (n,), w.dtype)
    kblock = jnp.concatenate(
        [-kr[..., half:], kr[..., :half], zeros(MLA_NOPE - MLA_ROPE), kr, zeros(LANES - MLA_NOPE - MLA_ROPE)], axis=-1)
    out = jnp.concatenate([gl, rq, rk, rv, rg, cb, cc, cx, q_lat, kv_lat, kblock], axis=-1)
    return out.astype(BF16)


def _prep_w_uq(w):
    r = w.shape[0]
    dq = MLA_NOPE + MLA_ROPE
    half = MLA_ROPE // 2
    w3 = w.reshape(r, MLA_HEADS, dq)
    pad = QPAD - dq
    wa = jnp.pad(w3, ((0, 0), (0, 0), (0, pad))).reshape(r, MLA_HEADS * QPAD)
    wb = jnp.concatenate(
        [jnp.zeros((r, MLA_HEADS, MLA_NOPE), w.dtype), -w3[:, :, MLA_NOPE + half:], w3[:, :, MLA_NOPE:MLA_NOPE + half],
         jnp.zeros((r, MLA_HEADS, pad), w.dtype)], axis=2).reshape(r, MLA_HEADS * QPAD)
    return wa.astype(BF16), wb.astype(BF16)


def _prep_w_ukv(w):
    r = w.shape[0]
    w3 = w.reshape(r, MLA_HEADS, MLA_NOPE + MLA_V)
    wk = jnp.pad(w3[:, :, :MLA_NOPE], ((0, 0), (0, 0), (0, QPAD - MLA_NOPE))).reshape(r, MLA_HEADS * QPAD)
    wv = jnp.pad(w3[:, :, MLA_NOPE:], ((0, 0), (0, 0), (0, QPAD - MLA_V))).reshape(r, MLA_HEADS * QPAD)
    return wk.astype(BF16), wv.T.astype(BF16)


def _prep_router(w_rg, b_rg, w_re, b_re):
    d = w_rg.shape[0]
    w = jnp.zeros((d, LANES), F32)
    w = w.at[:, :N_GROUPS].set(w_rg).at[:, ROUTE_E_OFF:ROUTE_E_OFF + N_EXPERTS].set(w_re)
    b = jnp.zeros((1, LANES), F32)
    b = b.at[0, :N_GROUPS].set(b_rg).at[0, ROUTE_E_OFF:ROUTE_E_OFF + N_EXPERTS].set(b_re)
    w_hi = w.astype(BF16)
    w_lo = (w - w_hi.astype(F32)).astype(BF16)
    return w_hi, w_lo, b


def kernel(x, c, positions, w_ada, b_ada, norm_mix_g, norm_ffn_g, w_in, mla_q_norm_g, mla_kv_norm_g, w_uq, w_ukv, w_o_mla, conv_w, w_o_conv, w_o_ret, w_mix_out, w_route_group, b_route_group, w_route_expert, b_route_expert, w_exp_gate, w_exp_up, w_exp_down, final_g):
    batch, seq, d = x.shape
    depth = w_in.shape[0]
    t = batch * seq
    x2 = x.reshape(t, d)
    mod = _ada_mod(c, w_ada, b_ada)
    rc, rs, mc, ms = _rope_tables(positions)
    w_in_all = _prep_w_in(w_in)
    for l in range(depth):
        mod3 = mod[l].reshape(batch * 6, 1, d)
        proj = _in_proj(x2, mod3, norm_mix_g[l].reshape(1, d), w_in_all, l, seq)
        wa, wb = _prep_w_uq(w_uq[l])
        wk, wv = _prep_w_ukv(w_ukv[l])
        q, k, v = _mla_prep(proj, mc, ms, mla_q_norm_g[l].reshape(1, -1), mla_kv_norm_g[l].reshape(1, -1), wa, wb, wk, wv,
                            batch, seq)
        att = _attention(q, k, v, batch, seq)
        ret = _retention(proj, rc, rs, batch, seq)
        wrh, wrl, br = _prep_router(w_route_group[l], b_route_group[l], w_route_expert[l], b_route_expert[l])
        cw = jnp.pad(conv_w[l], ((0, SUBLANES - CONV_K), (0, 0)))
        x2, h, rout, cnt = _mix(proj, att, ret, x2, mod3, norm_ffn_g[l].reshape(1, d), cw,
                           w_o_mla[l].astype(BF16), w_o_conv[l].astype(BF16), w_o_ret[l].astype(BF16),
                           w_mix_out[l].astype(BF16), wrh, wrl, br, seq)
        moe_out = _moe(h, rout, cnt, w_exp_gate, w_exp_up, w_exp_down, l)
        x2 = _combine(x2, moe_out, mod3, final_g.reshape(1, d), seq, final=(l == depth - 1))
    return x2.reshape(batch, seq, d)
```

```python
import functools

import jax
import jax.numpy as jnp
from jax import lax
from jax.experimental import pallas as pl
from jax.experimental.pallas import tpu as pltpu

F32 = jnp.float32
BF16 = jnp.bfloat16

D_MODEL = 1024
CHUNK = 64
EPS = 1e-6
ROPE_THETA = 10000.0
LOG2E = 1.4426950408889634

MLA_HEADS = 8
MLA_Q_RANK = 384
MLA_KV_RANK = 256
MLA_NOPE = 64
MLA_ROPE = 32
MLA_V = 64

CONV_WIDTH = 512
CONV_K = 3

RET_HEADS = 4
RET_DK = 128
RET_DV = 256

N_GROUPS = 4
EXPERTS_PER_GROUP = 8
N_EXPERTS = N_GROUPS * EXPERTS_PER_GROUP
EXPERT_FF = 512

LANES = 128
SUBLANES = 8
VMEM_LIMIT = 52 * 1024 * 1024

GATE_W = 3 * D_MODEL
RQ_OFF = GATE_W
RK_OFF = RQ_OFF + RET_HEADS * RET_DK
RV_OFF = RK_OFF + RET_HEADS * RET_DK
RG_OFF = RV_OFF + RET_HEADS * RET_DV
CONV_OFF = RG_OFF + RET_HEADS * RET_DV
MLA_OFF = CONV_OFF + 3 * CONV_WIDTH
MLA_W = MLA_Q_RANK + MLA_KV_RANK + LANES
PROJ_W = MLA_OFF + MLA_W
PROJ_N_TILES = 3
PROJ_TN = PROJ_W // PROJ_N_TILES
assert PROJ_TN * PROJ_N_TILES == PROJ_W and PROJ_TN % LANES == 0
assert MLA_OFF % MLA_W == 0 and CONV_OFF % (3 * CONV_WIDTH) == 0

QPAD = LANES
ROUTE_E_OFF = 32

ROW_TILE_PROJ = 1024
ROW_TILE = 512
ATT_TILE = 512
ATT_SUB = 256
ATT_GROUP = 4
LAZY_MAX_HEADROOM = 60.0
RET_BLOCK = 256
MOE_TILE = 256
ADA_TN = 768


def _params(sem, vmem=VMEM_LIMIT):
    return pltpu.CompilerParams(dimension_semantics=sem, vmem_limit_bytes=vmem)


def _const_spec(shape):
    nd = len(shape)
    return pl.BlockSpec(shape, lambda *_: (0,) * nd)


ROW_SPLIT = D_MODEL // LANES
assert ROW_SPLIT == SUBLANES


def _store_token_tiles(ref, base, rows, val):
    for s in range(ROW_SPLIT):
        ref[pl.ds(base + s, rows, stride=ROW_SPLIT), :] = val[:, s * LANES:(s + 1) * LANES]


def _load_token_tiles(ref, base, rows):
    return jnp.concatenate([ref[pl.ds(base + s, rows, stride=ROW_SPLIT), :] for s in range(ROW_SPLIT)], axis=1)


def _rms(x, g):
    return x * lax.rsqrt(jnp.mean(x * x, axis=-1, keepdims=True) + EPS) * g


def _sigmoid(x):
    return 1.0 / (1.0 + jnp.exp(-x))


def _ada_kernel(ct_ref, w_ref, b_ref, o_ref):
    ct = ct_ref[...]
    a = ct * _sigmoid(ct)
    w = w_ref[0]
    rows = [jnp.sum(w * a[:, b:b + 1], axis=0, keepdims=True) for b in range(ct.shape[1])]
    o_ref[0] = jnp.concatenate(rows, axis=0) + b_ref[0]


def _ada_mod(c, w_ada, b_ada):
    depth, d, n = w_ada.shape
    b = c.shape[0]
    return pl.pallas_call(
        _ada_kernel,
        grid=(depth, n // ADA_TN),
        in_specs=[
            _const_spec((d, b)),
            pl.BlockSpec((1, d, ADA_TN), lambda l, j: (l, 0, j)),
            pl.BlockSpec((1, 1, ADA_TN), lambda l, j: (l, 0, j)),
        ],
        out_specs=pl.BlockSpec((1, b, ADA_TN), lambda l, j: (l, 0, j)),
        out_shape=jax.ShapeDtypeStruct((depth, b, n), F32),
        compiler_params=_params(("arbitrary", "arbitrary")),
        name="ada_mod",
    )(c.T, w_ada, b_ada.reshape(depth, 1, n))


def _rope_kernel(pos_ref, f_ref, rc_ref, rs_ref, mc_ref, ms_ref):
    ang = pos_ref[...].astype(F32) * f_ref[...]
    cs = jnp.cos(ang)
    sn = jnp.sin(ang)
    lane = lax.broadcasted_iota(jnp.int32, ang.shape, 1)
    half = LANES // 2
    lo = lane < half
    rc_ref[...] = jnp.where(lo, cs, pltpu.roll(cs, half, 1))
    rs_ref[...] = jnp.where(lo, -sn, pltpu.roll(sn, half, 1))
    rope = (lane >= MLA_NOPE) & (lane < MLA_NOPE + MLA_ROPE)
    mc_ref[...] = jnp.where(lane < MLA_NOPE, 1.0, jnp.where(rope, cs, 0.0))
    ms_ref[...] = jnp.where(rope, sn, 0.0)


def _rope_tables(positions):
    t = positions.size
    tm = min(2048, t)
    inv_ret = ROPE_THETA ** (-jnp.arange(0, RET_DK, 2, dtype=F32) / RET_DK)
    inv_mla = ROPE_THETA ** (-jnp.arange(0, MLA_ROPE, 2, dtype=F32) / MLA_ROPE)
    freqs = jnp.concatenate([inv_ret, inv_mla, inv_mla, jnp.zeros((LANES - 96,), F32)]).reshape(1, LANES)
    spec = pl.BlockSpec((tm, LANES), lambda i: (i, 0))
    return pl.pallas_call(
        _rope_kernel,
        grid=(t // tm,),
        in_specs=[pl.BlockSpec((tm, 1), lambda i: (i, 0)), _const_spec((1, LANES))],
        out_specs=[spec] * 4,
        out_shape=[jax.ShapeDtypeStruct((t, LANES), F32)] * 4,
        compiler_params=_params(("arbitrary",)),
        name="rope_tables",
    )(positions.reshape(t, 1), freqs)


def _proj_kernel(x_ref, sh_ref, sc_ref, g_ref, w_ref, o_ref, h_scr):
    @pl.when(pl.program_id(1) == 0)
    def _():
        h = _rms(x_ref[...], g_ref[...]) * (1.0 + sc_ref[0]) + sh_ref[0]
        h_scr[...] = h.astype(BF16)

    o_ref[...] = jnp.dot(h_scr[...], w_ref[0], preferred_element_type=F32).astype(BF16)


def _in_proj(x2, mod3, g, w_all, layer, seq):
    t, d = x2.shape
    tm = min(ROW_TILE_PROJ, seq)
    per_b = seq // tm
    return pl.pallas_call(
        _proj_kernel,
        grid=(t // tm, PROJ_W // PROJ_TN),
        in_specs=[
            pl.BlockSpec((tm, d), lambda i, j: (i, 0)),
            pl.BlockSpec((1, 1, d), lambda i, j: ((i // per_b) * 6 + 0, 0, 0)),
            pl.BlockSpec((1, 1, d), lambda i, j: ((i // per_b) * 6 + 1, 0, 0)),
            _const_spec((1, d)),
            pl.BlockSpec((1, d, PROJ_TN), lambda i, j: (layer, 0, j)),
        ],
        out_specs=pl.BlockSpec((tm, PROJ_TN), lambda i, j: (i, j)),
        out_shape=jax.ShapeDtypeStruct((t, PROJ_W), BF16),
        scratch_shapes=[pltpu.VMEM((tm, d), BF16)],
        compiler_params=_params(("arbitrary", "arbitrary")),
        name="in_proj",
    )(x2, mod3, mod3, g, w_all)


def _mla_prep_kernel(in_ref, mc_ref, ms_ref, gq_ref, gkv_ref, wa_ref, wb_ref, wk_ref, wv_ref, q_ref, k_ref, v_ref):
    blk = in_ref[...].astype(F32)
    qn = _rms(blk[:, :MLA_Q_RANK], gq_ref[...]).astype(BF16)
    kvn = _rms(blk[:, MLA_Q_RANK:MLA_Q_RANK + MLA_KV_RANK], gkv_ref[...]).astype(BF16)
    kb = blk[:, MLA_Q_RANK + MLA_KV_RANK:]
    mc = mc_ref[...]
    ms = ms_ref[...]
    qa = jnp.dot(qn, wa_ref[...], preferred_element_type=F32)
    kn = jnp.dot(kvn, wk_ref[...], preferred_element_type=F32)
    lane = lax.broadcasted_iota(jnp.int32, kb.shape, 1)
    rope = (lane >= MLA_NOPE) & (lane < MLA_NOPE + MLA_ROPE)
    kpe = jnp.where(rope, kb * mc + pltpu.roll(kb, LANES // 2, 1) * ms, 0.0)
    scale = (MLA_NOPE + MLA_ROPE) ** -0.5 * LOG2E
    for h in range(MLA_HEADS):
        sl = slice(h * QPAD, (h + 1) * QPAD)
        qh = qa[:, sl]
        half = MLA_ROPE // 2
        partner = jnp.where(lane < MLA_NOPE + half, -pltpu.roll(qh, LANES - half, 1), pltpu.roll(qh, half, 1))
        q_ref[:, sl] = ((qh * mc + partner * ms) * scale).astype(BF16)
        k_ref[:, sl] = (kn[:, sl] + kpe).astype(BF16)
    vt = lax.dot_general(wv_ref[...], kvn, (((1,), (1,)), ((), ())), preferred_element_type=F32)
    row = lax.broadcasted_iota(jnp.int32, vt.shape, 0)
    v_ref[0, 0] = jnp.where(row % QPAD >= MLA_V, 1.0, vt).astype(BF16)


def _mla_prep(proj, mc, ms, gq, gkv, wa, wb, wk, wv, batch, seq):
    t = proj.shape[0]
    tm = min(ATT_TILE, seq)
    per_b = seq // tm
    hq = MLA_HEADS * QPAD
    hv = MLA_HEADS * QPAD
    return pl.pallas_call(
        _mla_prep_kernel,
        grid=(t // tm,),
        in_specs=[
            pl.BlockSpec((tm, MLA_W), lambda i: (i, MLA_OFF // MLA_W)),
            pl.BlockSpec((tm, LANES), lambda i: (i, 0)),
            pl.BlockSpec((tm, LANES), lambda i: (i, 0)),
            _const_spec((1, MLA_Q_RANK)),
            _const_spec((1, MLA_KV_RANK)),
            _const_spec((MLA_Q_RANK, hq)),
            _const_spec((MLA_Q_RANK, hq)),
            _const_spec((MLA_KV_RANK, hq)),
            _const_spec((hv, MLA_KV_RANK)),
        ],
        out_specs=[
            pl.BlockSpec((tm, hq), lambda i: (i, 0)),
            pl.BlockSpec((tm, hq), lambda i: (i, 0)),
            pl.BlockSpec((1, 1, hv, tm), lambda i: (i // per_b, i % per_b, 0, 0)),
        ],
        out_shape=[
            jax.ShapeDtypeStruct((t, hq), BF16),
            jax.ShapeDtypeStruct((t, hq), BF16),
            jax.ShapeDtypeStruct((batch, per_b, hv, tm), BF16),
        ],
        compiler_params=_params(("arbitrary",)),
        name="mla_prep",
    )(proj, mc, ms, gq, gkv, wa, wb, wk, wv)


def _attn_kernel(q_ref, k_ref, vt_ref, o_ref, m_scr, acc_scr, *, tile):
    qi = pl.program_id(2)
    nn_dims = (((1,), (1,)), ((), ()))

    def head_q(a):
        return q_ref[0, :, a * QPAD:(a + 1) * QPAD]

    def head_k(start, size, a):
        return k_ref[0, pl.ds(start, size), a * QPAD:(a + 1) * QPAD]

    def exact_step(j, src, dst, first, masked):
        start = pl.multiple_of(j * tile, tile)
        for a in range(2):
            st = lax.dot_general(head_k(start, tile, a), head_q(a), nn_dims, preferred_element_type=F32)
            if masked:
                key = lax.broadcasted_iota(jnp.int32, (tile, tile), 0)
                qry = lax.broadcasted_iota(jnp.int32, (tile, tile), 1)
                st = jnp.where((key // CHUNK) <= (qry // CHUNK), st, -jnp.inf)
            m_new = jnp.max(st, axis=0, keepdims=True)
            if not first:
                m_old = m_scr[src, a]
                m_new = jnp.maximum(m_old, m_new)
            p = jnp.exp2(st - m_new).astype(BF16)
            vt = vt_ref[0, j, a * QPAD:(a + 1) * QPAD, :]
            pv = jnp.dot(vt, p, preferred_element_type=F32)
            if not first:
                pv = pv + jnp.exp2(m_old - m_new) * acc_scr[src, a]
            acc_scr[dst, a] = pv
            m_scr[dst, a] = m_new

    exact_step(qi, 0, 0, first=True, masked=True)

    def lazy_step(j0, n_tiles, cur):
        nxt = 1 - cur
        n_sub = tile // ATT_SUB
        scores = {}
        for a in range(2):
            for t in range(n_tiles):
                start = pl.multiple_of((j0 + t) * tile, tile)
                for s in range(n_sub):
                    k = head_k(start + s * ATT_SUB, ATT_SUB, a)
                    scores[a, t, s] = lax.dot_general(k, head_q(a), nn_dims, preferred_element_type=F32)
        excess = jnp.zeros((1, tile), F32)
        for a in range(2):
            m_used = m_scr[cur, a]
            m_new = m_used
            pv = jnp.zeros((QPAD, tile), F32)
            for t in range(n_tiles):
                for s in range(n_sub):
                    st = scores[a, t, s]
                    m_new = jnp.maximum(m_new, jnp.max(st, axis=0, keepdims=True))
                    p = jnp.exp2(st - m_used).astype(BF16)
                    vt = vt_ref[0, j0 + t, a * QPAD:(a + 1) * QPAD, s * ATT_SUB:(s + 1) * ATT_SUB]
                    pv = pv + jnp.dot(vt, p, preferred_element_type=F32)
            acc_scr[nxt, a] = (acc_scr[cur, a] + pv) * jnp.exp2(m_used - m_new)
            m_scr[nxt, a] = m_new
            excess = jnp.maximum(excess, m_new - m_used)

        @pl.when(jnp.max(excess) > LAZY_MAX_HEADROOM)
        def _():
            def redo(t, carry):
                exact_step(j0 + t, jnp.where(t == 0, cur, nxt), nxt, first=False, masked=False)
                return carry

            lax.fori_loop(0, n_tiles, redo, 0)

        return nxt

    n_groups = qi // ATT_GROUP
    cur = lax.fori_loop(0, n_groups, lambda g, c: lazy_step(g * ATT_GROUP, ATT_GROUP, c), 0)
    done = n_groups * ATT_GROUP
    size = ATT_GROUP // 2
    while size >= 1:
        take = (qi - done) >= size

        @pl.when(take)
        def _(done=done, size=size, cur=cur):
            lazy_step(done, size, cur)

        cur = jnp.where(take, 1 - cur, cur)
        done = jnp.where(take, done + size, done)
        size //= 2

    lane = lax.broadcasted_iota(jnp.int32, (tile, LANES), 1)
    r0 = acc_scr[cur, 0].T
    r1 = acc_scr[cur, 1].T
    o0 = r0 / r0[:, MLA_V:MLA_V + 1]
    o1 = pltpu.roll(r1, MLA_V, 1) / r1[:, MLA_V:MLA_V + 1]
    o_ref[0] = jnp.where(lane < MLA_V, o0, o1).astype(BF16)


def _attention(q, k, vt, batch, seq):
    tile = min(ATT_TILE, seq)
    q3 = q.reshape(batch, seq, MLA_HEADS * QPAD)
    k3 = k.reshape(batch, seq, MLA_HEADS * QPAD)
    out = pl.pallas_call(
        functools.partial(_attn_kernel, tile=tile),
        grid=(batch, MLA_HEADS // 2, seq // tile),
        in_specs=[
            pl.BlockSpec((1, tile, 2 * QPAD), lambda b, h, i: (b, i, h)),
            pl.BlockSpec((1, seq, 2 * QPAD), lambda b, h, i: (b, 0, h)),
            pl.BlockSpec((1, seq // tile, 2 * QPAD, tile), lambda b, h, i: (b, 0, h, 0)),
        ],
        out_specs=pl.BlockSpec((1, tile, 2 * MLA_V), lambda b, h, i: (b, i, h)),
        out_shape=jax.ShapeDtypeStruct((batch, seq, MLA_HEADS * MLA_V), BF16),
        scratch_shapes=[
            pltpu.VMEM((2, 2, 1, tile), F32),
            pltpu.VMEM((2, 2, QPAD, tile), F32),
        ],
        compiler_params=_params(("arbitrary", "arbitrary", "arbitrary")),
        name="mla_attention",
    )(q3, k3, vt)
    return out.reshape(batch * seq, MLA_HEADS * MLA_V)


def _ret_kernel(lg_ref, q_ref, k_ref, v_ref, g_ref, rc_ref, rs_ref, o_ref, state_scr, decay_scr, *, blk):
    n = pl.program_id(1)

    @pl.when(n == 0)
    def _():
        state_scr[...] = jnp.zeros(state_scr.shape, F32)
        row = lax.broadcasted_iota(jnp.int32, (blk, blk), 0)
        col = lax.broadcasted_iota(jnp.int32, (blk, blk), 1)
        dist = jnp.abs(row - col).astype(F32)
        visible = (col // CHUNK) <= (row // CHUNK)
        for h in range(RET_HEADS):
            decay_scr[h] = jnp.where(visible, jnp.exp(lg_ref[h] * dist), 0.0)

    rc = rc_ref[...]
    rs = rs_ref[...]
    half = RET_DK // 2
    pos = lax.broadcasted_iota(jnp.int32, (blk, 1), 0).astype(F32)
    for h in range(RET_HEADS):
        lg = lg_ref[h]
        ksl = slice(h * RET_DK, (h + 1) * RET_DK)
        vsl = slice(h * RET_DV, (h + 1) * RET_DV)
        qf = q_ref[:, ksl].astype(F32)
        kf = k_ref[:, ksl].astype(F32)
        q = qf * rc + pltpu.roll(qf, half, 1) * rs
        k = (kf * rc + pltpu.roll(kf, half, 1) * rs) * (RET_DK ** -0.5)
        v = v_ref[:, vsl]
        q_dec = jnp.exp(lg * (pos + 1.0))
        k_dec = jnp.exp(lg * (blk - 1.0 - pos))

        s = lax.dot_general(q.astype(BF16), k.astype(BF16), (((1,), (1,)), ((), ())), preferred_element_type=F32)
        s = s * decay_scr[h]
        state = state_scr[h]
        o = jnp.dot(s.astype(BF16), v, preferred_element_type=F32)
        o = o + jnp.dot((q * q_dec).astype(BF16), state.astype(BF16), preferred_element_type=F32)
        kd = (k * k_dec).astype(BF16)
        upd = lax.dot_general(kd, v, (((0,), (0,)), ((), ())), preferred_element_type=F32)
        state_scr[h] = state * jnp.exp(lg * blk) + upd

        o = o * lax.rsqrt(jnp.mean(o * o, axis=-1, keepdims=True) + EPS)
        g = g_ref[:, vsl].astype(F32)
        o_ref[:, vsl] = (g * _sigmoid(g) * o).astype(BF16)


def _retention(proj, rc, rs, batch, seq):
    t = proj.shape[0]
    blk = min(RET_BLOCK, seq)
    per_b = seq // blk
    log_gamma = jnp.log(1.0 - 2.0 ** (-5.0 - jnp.arange(RET_HEADS, dtype=F32)))
    qk_w = RET_HEADS * RET_DK
    v_w = RET_HEADS * RET_DV
    row = lambda b, n: b * per_b + n
    return pl.pallas_call(
        functools.partial(_ret_kernel, blk=blk),
        grid=(batch, per_b),
        in_specs=[
            pl.BlockSpec(memory_space=pltpu.SMEM),
            pl.BlockSpec((blk, qk_w), lambda b, n: (row(b, n), RQ_OFF // qk_w)),
            pl.BlockSpec((blk, qk_w), lambda b, n: (row(b, n), RK_OFF // qk_w)),
            pl.BlockSpec((blk, v_w), lambda b, n: (row(b, n), RV_OFF // v_w)),
            pl.BlockSpec((blk, v_w), lambda b, n: (row(b, n), RG_OFF // v_w)),
            pl.BlockSpec((blk, LANES), lambda b, n: (row(b, n), 0)),
            pl.BlockSpec((blk, LANES), lambda b, n: (row(b, n), 0)),
        ],
        out_specs=pl.BlockSpec((blk, v_w), lambda b, n: (row(b, n), 0)),
        out_shape=jax.ShapeDtypeStruct((t, v_w), BF16),
        scratch_shapes=[pltpu.VMEM((RET_HEADS, RET_DK, RET_DV), F32), pltpu.VMEM((RET_HEADS, blk, blk), F32)],
        compiler_params=_params(("arbitrary", "arbitrary")),
        name="retention",
    )(log_gamma, proj, proj, proj, proj, rc, rs)


def _mix_kernel(gate_ref, conv_ref, att_ref, ret_ref, x_ref, gt_ref, sh_ref, sc_ref, gf_ref, cw_ref,
                wom_ref, woc_ref, wor_ref, wmo_ref, wrh_ref, wrl_ref, br_ref,
                xo_ref, h_ref, rout_ref, cnt_ref, u_scr, *, tm, per_b):
    i = pl.program_id(0)
    d = D_MODEL
    cw = CONV_WIDTH

    cb = conv_ref[:, 0:cw].astype(F32)
    u = conv_ref[:, cw:2 * cw].astype(F32) * conv_ref[:, 2 * cw:3 * cw].astype(F32)

    @pl.when(i % per_b == 0)
    def _():
        u_scr[0:SUBLANES, :] = jnp.zeros((SUBLANES, cw), F32)

    @pl.when(i % per_b != 0)
    def _():
        u_scr[0:SUBLANES, :] = u_scr[tm:tm + SUBLANES, :]

    u_scr[SUBLANES:SUBLANES + tm, :] = u
    y = cw_ref[2:3, :] * u
    y = y + cw_ref[1:2, :] * u_scr[SUBLANES - 1:SUBLANES - 1 + tm, :]
    y = y + cw_ref[0:1, :] * u_scr[SUBLANES - 2:SUBLANES - 2 + tm, :]
    z = (cb * y).astype(BF16)

    y_mla = jnp.dot(att_ref[...], wom_ref[...], preferred_element_type=F32)
    y_conv = jnp.dot(z, woc_ref[...], preferred_element_type=F32)
    y_ret = jnp.dot(ret_ref[...], wor_ref[...], preferred_element_type=F32)
    merged = _sigmoid(gate_ref[:, 0:d].astype(F32)) * y_mla
    merged = merged + _sigmoid(gate_ref[:, d:2 * d].astype(F32)) * y_conv
    merged = merged + _sigmoid(gate_ref[:, 2 * d:3 * d].astype(F32)) * y_ret
    x = x_ref[...] + gt_ref[0] * jnp.dot(merged.astype(BF16), wmo_ref[...], preferred_element_type=F32)
    xo_ref[...] = x

    h = _rms(x, gf_ref[...]) * (1.0 + sc_ref[0]) + sh_ref[0]
    _store_token_tiles(h_ref, 0, tm, h)

    h_hi = h.astype(BF16)
    h_lo = (h - h_hi.astype(F32)).astype(BF16)
    lg = jnp.dot(h_hi, wrh_ref[...], preferred_element_type=F32)
    lg = lg + jnp.dot(h_lo, wrh_ref[...], preferred_element_type=F32)
    lg = lg + jnp.dot(h_hi, wrl_ref[...], preferred_element_type=F32)
    lg = lg + br_ref[...]

    lane = lax.broadcasted_iota(jnp.int32, lg.shape, 1)
    neg = -jnp.inf
    gl = jnp.where(lane < N_GROUPS, lg, neg)
    gmax = jnp.max(gl, axis=-1, keepdims=True)
    g_val = 1.0 / jnp.sum(jnp.exp(gl - gmax), axis=-1, keepdims=True)
    g_idx = jnp.min(jnp.where(gl == gmax, lane, LANES), axis=-1, keepdims=True)
    in_group = (lane >= ROUTE_E_OFF) & (lane < ROUTE_E_OFF + N_EXPERTS)
    in_group = in_group & (((lane - ROUTE_E_OFF) // EXPERTS_PER_GROUP) == g_idx)
    el = jnp.where(in_group, lg, neg)
    m1 = jnp.max(el, axis=-1, keepdims=True)
    i1 = jnp.min(jnp.where(el == m1, lane, LANES), axis=-1, keepdims=True)
    el2 = jnp.where(lane == i1, neg, el)
    m2 = jnp.max(el2, axis=-1, keepdims=True)
    i2 = jnp.min(jnp.where(el2 == m2, lane, LANES), axis=-1, keepdims=True)
    r = jnp.exp(m2 - m1)
    w1 = g_val / (1.0 + r)
    w2 = g_val * r / (1.0 + r)
    e1 = (i1 - ROUTE_E_OFF).astype(F32)
    e2 = (i2 - ROUTE_E_OFF).astype(F32)
    rout = jnp.where(lane == 0, e1, jnp.where(lane == 1, e2, jnp.where(lane == 2, w1, jnp.where(lane == 3, w2, 0.0))))
    rout_ref[...] = rout.T[0:SUBLANES, :]

    hits = jnp.where(lane == i1, 1.0, 0.0) + jnp.where(lane == i2, 1.0, 0.0)

    @pl.when(i == 0)
    def _():
        cnt_ref[...] = jnp.zeros(cnt_ref.shape, F32)

    cnt_ref[0:1, :] = cnt_ref[0:1, :] + jnp.sum(hits, axis=0, keepdims=True)


def _mix(proj, att, ret, x2, mod3, gf, cw, wom, woc, wor, wmo, wrh, wrl, br, seq):
    t, d = x2.shape
    tm = min(ROW_TILE, seq)
    per_b = seq // tm
    row = lambda i: (i, 0)
    modspec = lambda k: pl.BlockSpec((1, 1, d), lambda i: ((i // per_b) * 6 + k, 0, 0))
    return pl.pallas_call(
        functools.partial(_mix_kernel, tm=tm, per_b=per_b),
        grid=(t // tm,),
        in_specs=[
            pl.BlockSpec((tm, GATE_W), row),
            pl.BlockSpec((tm, 3 * CONV_WIDTH), lambda i: (i, CONV_OFF // (3 * CONV_WIDTH))),
            pl.BlockSpec((tm, MLA_HEADS * MLA_V), row),
            pl.BlockSpec((tm, RET_HEADS * RET_DV), row),
            pl.BlockSpec((tm, d), row),
            modspec(2), modspec(3), modspec(4),
            _const_spec((1, d)),
            _const_spec((SUBLANES, CONV_WIDTH)),
            _const_spec(wom.shape), _const_spec(woc.shape), _const_spec(wor.shape), _const_spec(wmo.shape),
            _const_spec(wrh.shape), _const_spec(wrl.shape), _const_spec((1, LANES)),
        ],
        out_specs=[pl.BlockSpec((tm, d), row), pl.BlockSpec((tm * ROW_SPLIT, LANES), row),
                   pl.BlockSpec((SUBLANES, tm), lambda i: (0, i)), _const_spec((SUBLANES, LANES))],
        out_shape=[
            jax.ShapeDtypeStruct((t, d), F32),
            jax.ShapeDtypeStruct((t * ROW_SPLIT, LANES), F32),
            jax.ShapeDtypeStruct((SUBLANES, t), F32),
            jax.ShapeDtypeStruct((SUBLANES, LANES), F32),
        ],
        scratch_shapes=[pltpu.VMEM((tm + SUBLANES, CONV_WIDTH), F32)],
        compiler_params=_params(("arbitrary",)),
        name="mix_router",
    )(proj, proj, att, ret, x2, mod3, mod3, mod3, gf, cw, wom, woc, wor, wmo, wrh, wrl, br)


def _moe_kernel(te_ref, nv_ref, meta_ref, meta_next_ref, wt_ref, wg_ref, wu_ref, wd_ref, h_hbm, out_hbm,
                xbuf, ybuf, wgb, wub, wdb, gsem, ssem, *, tm):
    i = pl.program_id(0)
    nt = pl.num_programs(0)
    slot = i % 2
    nv = nv_ref[i]
    prev = jnp.maximum(i - 1, 0)
    prev2 = jnp.maximum(i - 2, 0)

    def gather_copy(tok, s, r):
        src = h_hbm.at[pl.ds(pl.multiple_of(tok * ROW_SPLIT, ROW_SPLIT), ROW_SPLIT)]
        return pltpu.make_async_copy(src, xbuf.at[pl.ds((s * tm + r) * ROW_SPLIT, ROW_SPLIT)], gsem.at[s])

    def scatter_copy(dst, s, r):
        dst_rows = out_hbm.at[pl.ds(pl.multiple_of(dst * ROW_SPLIT, ROW_SPLIT), ROW_SPLIT)]
        return pltpu.make_async_copy(ybuf.at[pl.ds((s * tm + r) * ROW_SPLIT, ROW_SPLIT)], dst_rows, ssem.at[s])

    def start_gather(meta, s, lo=0, hi=tm):
        for r in range(lo, hi):
            gather_copy(meta[0, 0, r], s, r).start(priority=r % 2)

    def wait_gather(s):
        for r in range(tm):
            gather_copy(0, s, r).wait()

    def start_scatter(s):
        for r in range(tm):
            scatter_copy(meta_ref[0, 0, tm + r], s, r).start(priority=r % 2)

    def wait_scatter(s):
        for r in range(tm):
            scatter_copy(0, s, r).wait()

    def for_slot(fn):
        @pl.when(slot == 0)
        def _():
            fn(0)

        @pl.when(slot == 1)
        def _():
            fn(1)

    @pl.when(i == 0)
    def _():
        ybuf[...] = jnp.zeros(ybuf.shape, F32)
        half = tm * ROW_SPLIT
        for s in range(2):
            pad_rows = out_hbm.at[pl.ds(out_hbm.shape[0] - (2 - s) * half, half)]
            init = pltpu.make_async_copy(ybuf.at[pl.ds(s * half, half)], pad_rows, ssem.at[s])
            init.start()
            init.wait()
        start_gather(meta_ref, 0)

    @pl.when((i >= 2) & (nv_ref[prev2] > 0))
    def _():
        for_slot(wait_scatter)

    @pl.when((i == 0) | (te_ref[i] != te_ref[prev]))
    def _():
        wgb[...] = wg_ref[0].astype(BF16)
        wub[...] = wu_ref[0].astype(BF16)
        wdb[...] = wd_ref[0].astype(BF16)

    def active(s):
        wait_gather(s)
        x = _load_token_tiles(xbuf, s * tm * ROW_SPLIT, tm).astype(BF16)
        q4 = tm // 4
        start_gather(meta_next_ref, 1 - s, 0, q4)
        g = jnp.dot(x, wgb[...], preferred_element_type=F32)
        start_gather(meta_next_ref, 1 - s, q4, 2 * q4)
        u = jnp.dot(x, wub[...], preferred_element_type=F32)
        start_gather(meta_next_ref, 1 - s, 2 * q4, 3 * q4)
        hid = (g * _sigmoid(g) * u).astype(BF16)
        w_col = jnp.broadcast_to(wt_ref[0], (LANES, tm)).T[:, 0:1]
        start_gather(meta_next_ref, 1 - s, 3 * q4, tm)
        y = jnp.dot(hid, wdb[...], preferred_element_type=F32) * w_col
        _store_token_tiles(ybuf, s * tm * ROW_SPLIT, tm, y)
        start_scatter(s)

    @pl.when(nv > 0)
    def _():
        for_slot(active)

    @pl.when((i > 0) & (nv == 0) & (nv_ref[prev] > 0))
    def _():
        for_slot(wait_gather)

    @pl.when((i == nt - 1) & (nv_ref[prev] > 0))
    def _():
        for_slot(lambda s: wait_scatter(1 - s))


def _route_tables(rout, cnt, t, tm, nt):
    e = rout[0:2].astype(jnp.int32).reshape(-1)
    w = rout[2:4].reshape(-1)
    n_slot = 2 * t
    perm = jnp.argsort(e).astype(jnp.int32)
    counts = cnt[0, ROUTE_E_OFF:ROUTE_E_OFF + N_EXPERTS].astype(jnp.int32)
    nt_e = (counts + tm - 1) // tm
    tile_end = jnp.cumsum(nt_e)
    tile_start = tile_end - nt_e
    off = jnp.cumsum(counts) - counts
    total = tile_end[-1]
    i = jnp.arange(nt, dtype=jnp.int32)
    i_act = jnp.minimum(i, total - 1)
    e_i = jnp.minimum(jnp.sum((tile_end[None, :] <= i_act[:, None]).astype(jnp.int32), axis=1), N_EXPERTS - 1)
    sel = e_i[:, None] == jnp.arange(N_EXPERTS, dtype=jnp.int32)[None, :]
    pick = lambda v: jnp.sum(jnp.where(sel, v[None, :], 0), axis=1)
    j = i - pick(tile_start)
    nvalid = jnp.where(i < total, jnp.clip(pick(counts) - j * tm, 0, tm), 0).astype(jnp.int32)
    r = jnp.arange(tm, dtype=jnp.int32)
    pos = jnp.clip((pick(off) + j * tm)[:, None] + r[None, :], 0, n_slot - 1)
    slot = perm[pos]
    valid = r[None, :] < nvalid[:, None]
    tok = jnp.where(slot >= t, slot - t, slot)
    dst = jnp.where(valid, slot, n_slot + (i % 2)[:, None] * tm + r[None, :])
    wt = jnp.where(valid, w[slot], 0.0)
    meta = jnp.concatenate([tok, dst], axis=1).astype(jnp.int32).reshape(nt, 1, 2 * tm)
    return e_i, nvalid, meta, wt.reshape(nt, 1, tm)


def _moe(h, rout, cnt, wg, wu, wd, layer):
    t, d = h.shape[0] // ROW_SPLIT, D_MODEL
    tm = min(MOE_TILE, t)
    nt = (2 * t) // tm + N_EXPERTS
    e_i, nvalid, meta, wt = _route_tables(rout, cnt, t, tm, nt)
    f = wg.shape[-1]
    grid_spec = pltpu.PrefetchScalarGridSpec(
        num_scalar_prefetch=2,
        grid=(nt,),
        in_specs=[
            pl.BlockSpec((1, 1, 2 * tm), lambda i, te, nv: (i, 0, 0), memory_space=pltpu.SMEM),
            pl.BlockSpec((1, 1, 2 * tm), lambda i, te, nv: (jnp.minimum(i + 1, nt - 1), 0, 0), memory_space=pltpu.SMEM),
            pl.BlockSpec((1, 1, tm), lambda i, te, nv: (i, 0, 0)),
            pl.BlockSpec((1, 1, d, f), lambda i, te, nv: (layer, te[i], 0, 0)),
            pl.BlockSpec((1, 1, d, f), lambda i, te, nv: (layer, te[i], 0, 0)),
            pl.BlockSpec((1, 1, f, d), lambda i, te, nv: (layer, te[i], 0, 0)),
            pl.BlockSpec(memory_space=pl.ANY),
        ],
        out_specs=pl.BlockSpec(memory_space=pl.ANY),
        scratch_shapes=[
            pltpu.VMEM((2 * tm * ROW_SPLIT, LANES), F32),
            pltpu.VMEM((2 * tm * ROW_SPLIT, LANES), F32),
            pltpu.VMEM((d, f), BF16),
            pltpu.VMEM((d, f), BF16),
            pltpu.VMEM((f, d), BF16),
            pltpu.SemaphoreType.DMA((2,)),
            pltpu.SemaphoreType.DMA((2,)),
        ],
    )

    def kern(te_ref, nv_ref, meta_ref, meta_next_ref, wt_ref, wg_ref, wu_ref, wd_ref, h_hbm, out_hbm, *scratch):
        _moe_kernel(te_ref, nv_ref, meta_ref, meta_next_ref, wt_ref, wg_ref.at[0], wu_ref.at[0], wd_ref.at[0],
                    h_hbm, out_hbm, *scratch, tm=tm)

    return pl.pallas_call(
        kern,
        grid_spec=grid_spec,
        out_shape=jax.ShapeDtypeStruct(((2 * t + 2 * tm) * ROW_SPLIT, LANES), F32),
        compiler_params=_params(("arbitrary",)),
        name="moe_experts",
    )(e_i, nvalid, meta, meta, wt, wg, wu, wd, h)


def _combine_kernel(x_ref, a_ref, b_ref, gt_ref, g_ref, o_ref, *, final):
    tm = x_ref.shape[0]
    moe = _load_token_tiles(a_ref, 0, tm) + _load_token_tiles(b_ref, 0, tm)
    x = x_ref[...] + gt_ref[0] * moe
    if final:
        x = _rms(x, g_ref[...])
    o_ref[...] = x


def _combine(x2, moe_out, mod3, final_g, seq, final):
    t, d = x2.shape
    tm = min(512, seq)
    per_b = seq // tm
    return pl.pallas_call(
        functools.partial(_combine_kernel, final=final),
        grid=(t // tm,),
        in_specs=[
            pl.BlockSpec((tm, d), lambda i: (i, 0)),
            pl.BlockSpec((tm * ROW_SPLIT, LANES), lambda i: (i, 0)),
            pl.BlockSpec((tm * ROW_SPLIT, LANES), lambda i: (t // tm + i, 0)),
            pl.BlockSpec((1, 1, d), lambda i: ((i // per_b) * 6 + 5, 0, 0)),
            _const_spec((1, d)),
        ],
        out_specs=pl.BlockSpec((tm, d), lambda i: (i, 0)),
        out_shape=jax.ShapeDtypeStruct((t, d), F32),
        compiler_params=_params(("arbitrary",)),
        name="combine",
    )(x2, moe_out, moe_out, mod3, final_g)


def _prep_w_in(w):
    sizes = (MLA_Q_RANK, MLA_KV_RANK, MLA_ROPE, CONV_WIDTH, CONV_WIDTH, CONV_WIDTH,
             RET_HEADS * RET_DK, RET_HEADS * RET_DK, RET_HEADS * RET_DV, RET_HEADS * RET_DV, 3 * D_MODEL)
    parts = []
    start = 0
    for size in sizes:
        parts.append(w[..., start:start + size])
        start += size
    q_lat, kv_lat, kr, cb, cc, cx, rq, rk, rv, rg, gl = parts
    half = MLA_ROPE // 2
    zeros = lambda n: jnp.zeros(w.shape[:-1] + (n,), w.dtype)
    kblock = jnp.concatenate(
        [-kr[..., half:], kr[..., :half], zeros(MLA_NOPE - MLA_ROPE), kr, zeros(LANES - MLA_NOPE - MLA_ROPE)], axis=-1)
    out = jnp.concatenate([gl, rq, rk, rv, rg, cb, cc, cx, q_lat, kv_lat, kblock], axis=-1)
    return out.astype(BF16)


def _prep_w_uq(w):
    r = w.shape[0]
    dq = MLA_NOPE + MLA_ROPE
    half = MLA_ROPE // 2
    w3 = w.reshape(r, MLA_HEADS, dq)
    pad = QPAD - dq
    wa = jnp.pad(w3, ((0, 0), (0, 0), (0, pad))).reshape(r, MLA_HEADS * QPAD)
    wb = jnp.concatenate(
        [jnp.zeros((r, MLA_HEADS, MLA_NOPE), w.dtype), -w3[:, :, MLA_NOPE + half:], w3[:, :, MLA_NOPE:MLA_NOPE + half],
         jnp.zeros((r, MLA_HEADS, pad), w.dtype)], axis=2).reshape(r, MLA_HEADS * QPAD)
    return wa.astype(BF16), wb.astype(BF16)


def _prep_w_ukv(w):
    r = w.shape[0]
    w3 = w.reshape(r, MLA_HEADS, MLA_NOPE + MLA_V)
    wk = jnp.pad(w3[:, :, :MLA_NOPE], ((0, 0), (0, 0), (0, QPAD - MLA_NOPE))).reshape(r, MLA_HEADS * QPAD)
    wv = jnp.pad(w3[:, :, MLA_NOPE:], ((0, 0), (0, 0), (0, QPAD - MLA_V))).reshape(r, MLA_HEADS * QPAD)
    return wk.astype(BF16), wv.T.astype(BF16)


def _prep_router(w_rg, b_rg, w_re, b_re):
    d = w_rg.shape[0]
    w = jnp.zeros((d, LANES), F32)
    w = w.at[:, :N_GROUPS].set(w_rg).at[:, ROUTE_E_OFF:ROUTE_E_OFF + N_EXPERTS].set(w_re)
    b = jnp.zeros((1, LANES), F32)
    b = b.at[0, :N_GROUPS].set(b_rg).at[0, ROUTE_E_OFF:ROUTE_E_OFF + N_EXPERTS].set(b_re)
    w_hi = w.astype(BF16)
    w_lo = (w - w_hi.astype(F32)).astype(BF16)
    return w_hi, w_lo, b


def kernel(x, c, positions, w_ada, b_ada, norm_mix_g, norm_ffn_g, w_in, mla_q_norm_g, mla_kv_norm_g, w_uq, w_ukv, w_o_mla, conv_w, w_o_conv, w_o_ret, w_mix_out, w_route_group, b_route_group, w_route_expert, b_route_expert, w_exp_gate, w_exp_up, w_exp_down, final_g):
    batch, seq, d = x.shape
    depth = w_in.shape[0]
    t = batch * seq
    x2 = x.reshape(t, d)
    mod = _ada_mod(c, w_ada, b_ada)
    rc, rs, mc, ms = _rope_tables(positions)
    w_in_all = _prep_w_in(w_in)
    for l in range(depth):
        mod3 = mod[l].reshape(batch * 6, 1, d)
        proj = _in_proj(x2, mod3, norm_mix_g[l].reshape(1, d), w_in_all, l, seq)
        wa, wb = _prep_w_uq(w_uq[l])
        wk, wv = _prep_w_ukv(w_ukv[l])
        q, k, v = _mla_prep(proj, mc, ms, mla_q_norm_g[l].reshape(1, -1), mla_kv_norm_g[l].reshape(1, -1), wa, wb, wk, wv,
                            batch, seq)
        att = _attention(q, k, v, batch, seq)
        ret = _retention(proj, rc, rs, batch, seq)
        wrh, wrl, br = _prep_router(w_route_group[l], b_route_group[l], w_route_expert[l], b_route_expert[l])
        cw = jnp.pad(conv_w[l], ((0, SUBLANES - CONV_K), (0, 0)))
        x2, h, rout, cnt = _mix(proj, att, ret, x2, mod3, norm_ffn_g[l].reshape(1, d), cw,
                           w_o_mla[l].astype(BF16), w_o_conv[l].astype(BF16), w_o_ret[l].astype(BF16),
                           w_mix_out[l].astype(BF16), wrh, wrl, br, seq)
        moe_out = _moe(h, rout, cnt, w_exp_gate, w_exp_up, w_exp_down, l)
        x2 = _combine(x2, moe_out, mod3, final_g.reshape(1, d), seq, final=(l == depth - 1))
    return x2.reshape(batch, seq, d)
```
